```python
import math
import jax, jax.numpy as jnp
from jax import lax
import numpy as np

D_MODEL = 4096
BATCH = 2
SEQ = 4096
DEPTH = 2

MIX_HALF = D_MODEL // 2
SB_HEAD_DIM = 128
SB_HEADS = MIX_HALF // SB_HEAD_DIM
SB_BLOCK = 128
GM_GROUP_DIM = 128
GM_GROUPS = MIX_HALF // GM_GROUP_DIM
GM_CHUNK = 128
SSM_WIDTH = D_MODEL
SSM_GROUP = 16
SSM_GROUPS = SSM_WIDTH // SSM_GROUP
SSM_STATE = 64
DT_MIN = 1e-3
DT_MAX = 1e-1
FFN_DENSE = 11008
N_EXPERTS = 8
TOP_K = 2
FFN_EXPERT = 4096
EPS = 1e-6

kernel_name = 'hybrid_sb_gmlp_s5_moe_adaln'


def rms_norm(x, g):
    xf = x.astype(jnp.float32)
    y = xf * lax.rsqrt(jnp.mean(xf * xf, axis=-1, keepdims=True) + EPS)
    return (y * g.astype(jnp.float32)).astype(x.dtype)


def ada_params(c, ada_w, ada_b):
    m = (jax.nn.silu(c) @ ada_w + ada_b)[:, None, :]
    shift, scale, gate = jnp.split(m, 3, axis=-1)
    return shift, scale, gate


def modulated_norm(x, g, shift, scale):
    return rms_norm(x, g) * (1.0 + scale) + shift


def swiglu(h, w_gate, w_up, w_down):
    return (jax.nn.silu(h @ w_gate) * (h @ w_up)) @ w_down


def stick_breaking_attention(q, k, v):
    bsz, seq, nh, dh = q.shape
    nb = seq // SB_BLOCK
    qb = q.reshape(bsz, nb, SB_BLOCK, nh, dh).transpose(1, 0, 3, 2, 4).astype(jnp.float32)
    kf = k.transpose(0, 2, 1, 3).astype(jnp.float32)
    vf = v.transpose(0, 2, 1, 3).astype(jnp.float32)
    key_pos = jnp.arange(seq)
    inv_sqrt_d = 1.0 / math.sqrt(dh)

    def one_block(args):
        q_blk, blk_idx = args
        q_pos = blk_idx * SB_BLOCK + jnp.arange(SB_BLOCK)
        z = jnp.einsum('bhqd,bhkd->bhqk', q_blk, kf) * inv_sqrt_d
        past = key_pos[None, :] < q_pos[:, None]
        log_beta = jax.nn.log_sigmoid(z)
        log_one_minus = jnp.where(past, jax.nn.log_sigmoid(-z), 0.0)
        after = lax.cumsum(log_one_minus, axis=3, reverse=True) - log_one_minus
        w = jnp.where(past, jnp.exp(log_beta + after), 0.0)
        return jnp.einsum('bhqk,bhkd->bhqd', w, vf)

    out = lax.map(one_block, (qb, jnp.arange(nb)))
    return out.transpose(1, 0, 3, 2, 4).reshape(bsz, seq, nh, dh).astype(q.dtype)


def chunked_spatial_gating(z1, z2, ln_g, w_s, b_s):
    bsz, seq, ng, cd = z1.shape
    nc = seq // GM_CHUNK
    u = jax.nn.gelu(z1)
    vf = jax.nn.gelu(z2).astype(jnp.float32)
    mu = jnp.mean(vf, axis=-1, keepdims=True)
    var = jnp.mean(jnp.square(vf - mu), axis=-1, keepdims=True)
    vn = (vf - mu) * lax.rsqrt(var + EPS) * ln_g.astype(jnp.float32)
    vn = vn.reshape(bsz, nc, GM_CHUNK, ng, cd)
    causal = jnp.tril(jnp.ones((GM_CHUNK, GM_CHUNK), dtype=bool))
    w = jnp.where(causal[None], w_s.astype(jnp.float32), 0.0)
    mixed = jnp.einsum('gts,bnsgc->bntgc', w, vn) + b_s.astype(jnp.float32).T[None, None, :, :, None]
    return u * mixed.reshape(bsz, seq, ng, cd).astype(u.dtype)


def _ssm_combine(left, right):
    a1r, a1i, b1r, b1i = left
    a2r, a2i, b2r, b2i = right
    return (a2r * a1r - a2i * a1i,
            a2r * a1i + a2i * a1r,
            a2r * b1r - a2i * b1i + b2r,
            a2r * b1i + a2i * b1r + b2i)


def s5_ssm(u, lam_re, lam_im, log_dt, b_re, b_im, c_re, c_im, d_skip):
    bsz, seq, width = u.shape
    uf = u.astype(jnp.float32).reshape(bsz, seq, SSM_GROUPS, SSM_GROUP)
    lr = jnp.minimum(lam_re.astype(jnp.float32), -1e-4)
    li = lam_im.astype(jnp.float32)
    dt = jnp.exp(log_dt.astype(jnp.float32))[:, None]
    mag = jnp.exp(lr * dt)
    a_re = mag * jnp.cos(li * dt)
    a_im = mag * jnp.sin(li * dt)
    den = lr * lr + li * li
    nr = a_re - 1.0
    f_re = (nr * lr + a_im * li) / den
    f_im = (a_im * lr - nr * li) / den
    br = b_re.astype(jnp.float32)
    bi = b_im.astype(jnp.float32)
    bb_re = f_re[:, :, None] * br - f_im[:, :, None] * bi
    bb_im = f_re[:, :, None] * bi + f_im[:, :, None] * br
    bu_re = jnp.einsum('gpc,bsgc->bsgp', bb_re, uf)
    bu_im = jnp.einsum('gpc,bsgc->bsgp', bb_im, uf)
    full = (bsz, seq, SSM_GROUPS, SSM_STATE)
    a_re_t = jnp.broadcast_to(a_re[None, None], full)
    a_im_t = jnp.broadcast_to(a_im[None, None], full)
    _, _, x_re, x_im = lax.associative_scan(_ssm_combine, (a_re_t, a_im_t, bu_re, bu_im), axis=1)
    y = (jnp.einsum('gcp,bsgp->bsgc', c_re.astype(jnp.float32), x_re)
         - jnp.einsum('gcp,bsgp->bsgc', c_im.astype(jnp.float32), x_im))
    y = y.reshape(bsz, seq, width) + d_skip.astype(jnp.float32) * u.astype(jnp.float32)
    return y.astype(u.dtype)


def moe_swiglu(h, w_router, w_gate, w_up, w_down):
    logits = jnp.einsum('bsd,de->bse', h, w_router).astype(jnp.float32)
    top_val, top_idx = lax.top_k(logits, TOP_K)
    probs = jax.nn.softmax(top_val, axis=-1)
    gates = jnp.sum(jax.nn.one_hot(top_idx, N_EXPERTS, dtype=jnp.float32) * probs[..., None], axis=-2)
    out = jnp.zeros_like(h)
    for e in range(N_EXPERTS):
        out = out + gates[..., e:e + 1].astype(h.dtype) * swiglu(h, w_gate[e], w_up[e], w_down[e])
    return out


def setup_inputs(seed: int = 0) -> dict:
    key = jax.random.key(seed)
    keys = iter(jax.random.split(key, 64))
    n_e = (DEPTH + 1) // 2
    n_o = DEPTH // 2
    D = D_MODEL

    def nrm(shape, scale):
        return jax.random.normal(next(keys), shape, jnp.float32) * scale

    def gain(shape):
        return 1.0 + nrm(shape, 0.02)

    def ada_w(n):
        return nrm((n, D, 3 * D), 0.2 * D ** -0.5)

    def ada_b(n):
        return jnp.concatenate([nrm((n, 2 * D), 0.02), 1.0 + nrm((n, D), 0.02)], axis=-1)

    lam_im0 = jnp.pi * jnp.arange(SSM_STATE, dtype=jnp.float32)
    inp = {
        'x': nrm((BATCH, SEQ, D), 1.0),
        'c': nrm((BATCH, D), 1.0),
        'mix0_norm_g': gain((n_e, D)),
        'mix0_ada_w': ada_w(n_e),
        'mix0_ada_b': ada_b(n_e),
        'mix0_w_in': nrm((n_e, D, 5 * MIX_HALF), D ** -0.5),
        'gm_ln_g': gain((n_e, GM_GROUPS, GM_GROUP_DIM)),
        'gm_w_s': nrm((n_e, GM_GROUPS, GM_CHUNK, GM_CHUNK), GM_CHUNK ** -0.5),
        'gm_b_s': gain((n_e, GM_GROUPS, GM_CHUNK)),
        'mix0_w_out': nrm((n_e, 2 * MIX_HALF, D), (2 * MIX_HALF) ** -0.5),
        'ffn0_norm_g': gain((n_e, D)),
        'ffn0_ada_w': ada_w(n_e),
        'ffn0_ada_b': ada_b(n_e),
        'ffn0_w_gate': nrm((n_e, D, FFN_DENSE), D ** -0.5),
        'ffn0_w_up': nrm((n_e, D, FFN_DENSE), D ** -0.5),
        'ffn0_w_down': nrm((n_e, FFN_DENSE, D), FFN_DENSE ** -0.5),
        'mix1_norm_g': gain((n_o, D)),
        'mix1_ada_w': ada_w(n_o),
        'mix1_ada_b': ada_b(n_o),
        'ssm_w_in': nrm((n_o, D, SSM_WIDTH), D ** -0.5),
        'ssm_lam_re': -0.5 + nrm((n_o, SSM_GROUPS, SSM_STATE), 0.01),
        'ssm_lam_im': lam_im0 + nrm((n_o, SSM_GROUPS, SSM_STATE), 0.01),
        'ssm_log_dt': jax.random.uniform(next(keys), (n_o, SSM_GROUPS), jnp.float32,
                                         minval=math.log(DT_MIN), maxval=math.log(DT_MAX)),
        'ssm_b_re': nrm((n_o, SSM_GROUPS, SSM_STATE, SSM_GROUP), (2 * SSM_GROUP) ** -0.5),
        'ssm_b_im': nrm((n_o, SSM_GROUPS, SSM_STATE, SSM_GROUP), (2 * SSM_GROUP) ** -0.5),
        'ssm_c_re': nrm((n_o, SSM_GROUPS, SSM_GROUP, SSM_STATE), 0.5),
        'ssm_c_im': nrm((n_o, SSM_GROUPS, SSM_GROUP, SSM_STATE), 0.5),
        'ssm_d': nrm((n_o, SSM_WIDTH), 0.5),
        'glu_w_a': nrm((n_o, SSM_WIDTH, D), SSM_WIDTH ** -0.5),
        'glu_w_b': nrm((n_o, SSM_WIDTH, D), SSM_WIDTH ** -0.5),
        'moe_norm_g': gain((n_o, D)),
        'moe_ada_w': ada_w(n_o),
        'moe_ada_b': ada_b(n_o),
        'moe_w_router': nrm((n_o, D, N_EXPERTS), D ** -0.5),
        'moe_w_gate': nrm((n_o, N_EXPERTS, D, FFN_EXPERT), D ** -0.5),
        'moe_w_up': nrm((n_o, N_EXPERTS, D, FFN_EXPERT), D ** -0.5),
        'moe_w_down': nrm((n_o, N_EXPERTS, FFN_EXPERT, D), FFN_EXPERT ** -0.5),
        'final_norm_g': gain((D,)),
    }
    return inp


def reference(x, c, mix0_norm_g, mix0_ada_w, mix0_ada_b, mix0_w_in, gm_ln_g, gm_w_s, gm_b_s, mix0_w_out,
              ffn0_norm_g, ffn0_ada_w, ffn0_ada_b, ffn0_w_gate, ffn0_w_up, ffn0_w_down,
              mix1_norm_g, mix1_ada_w, mix1_ada_b, ssm_w_in, ssm_lam_re, ssm_lam_im, ssm_log_dt,
              ssm_b_re, ssm_b_im, ssm_c_re, ssm_c_im, ssm_d, glu_w_a, glu_w_b,
              moe_norm_g, moe_ada_w, moe_ada_b, moe_w_router, moe_w_gate, moe_w_up, moe_w_down,
              final_norm_g):
    bsz, seq, _ = x.shape
    for layer in range(DEPTH):
        i = layer // 2
        if layer % 2 == 0:
            shift, scale, gate = ada_params(c, mix0_ada_w[i], mix0_ada_b[i])
            h = modulated_norm(x, mix0_norm_g[i], shift, scale)
            q, k, v, z1, z2 = jnp.split(h @ mix0_w_in[i], 5, axis=-1)
            sb_shape = (bsz, seq, SB_HEADS, SB_HEAD_DIM)
            gm_shape = (bsz, seq, GM_GROUPS, GM_GROUP_DIM)
            a_out = stick_breaking_attention(q.reshape(sb_shape), k.reshape(sb_shape), v.reshape(sb_shape))
            b_out = chunked_spatial_gating(z1.reshape(gm_shape), z2.reshape(gm_shape),
                                           gm_ln_g[i], gm_w_s[i], gm_b_s[i])
            mixed = jnp.concatenate([a_out.reshape(bsz, seq, MIX_HALF), b_out.reshape(bsz, seq, MIX_HALF)], axis=-1)
            x = x + gate * (mixed @ mix0_w_out[i])
            shift, scale, gate = ada_params(c, ffn0_ada_w[i], ffn0_ada_b[i])
            h = modulated_norm(x, ffn0_norm_g[i], shift, scale)
            x = x + gate * swiglu(h, ffn0_w_gate[i], ffn0_w_up[i], ffn0_w_down[i])
        else:
            shift, scale, gate = ada_params(c, mix1_ada_w[i], mix1_ada_b[i])
            h = modulated_norm(x, mix1_norm_g[i], shift, scale)
            y = s5_ssm(h @ ssm_w_in[i], ssm_lam_re[i], ssm_lam_im[i], ssm_log_dt[i],
                       ssm_b_re[i], ssm_b_im[i], ssm_c_re[i], ssm_c_im[i], ssm_d[i])
            y = jax.nn.gelu(y)
            x = x + gate * ((y @ glu_w_a[i]) * jax.nn.sigmoid(y @ glu_w_b[i]))
            shift, scale, gate = ada_params(c, moe_ada_w[i], moe_ada_b[i])
            h = modulated_norm(x, moe_norm_g[i], shift, scale)
            x = x + gate * moe_swiglu(h, moe_w_router[i], moe_w_gate[i], moe_w_up[i], moe_w_down[i])
    return rms_norm(x, final_norm_g)
```

```python
import functools
import math

import jax
import jax.numpy as jnp
from jax import lax
from jax.experimental import pallas as pl
from jax.experimental.pallas import tpu as pltpu

EPS = 1e-6
SB_HEAD_DIM = 128
SB_TILE = 256
GM_GROUP_DIM = 128
GM_CHUNK = 128
GM_ROWS = 512
SSM_GROUP = 16
SSM_STATE = 64
SSM_CHUNK = 8
LANES = 128
N_EXPERTS = 8
MOE_TILE = 256
SB_SKIP_EXPONENT = 104.0
VMEM_LIMIT_BYTES = 56 * 1024 * 1024

BF16 = jnp.bfloat16
F32 = jnp.float32


def _params(*sem):
    return pltpu.CompilerParams(dimension_semantics=sem, vmem_limit_bytes=VMEM_LIMIT_BYTES)


def _dot(a, b):
    return jnp.dot(a, b, preferred_element_type=F32)


def _ada_kernel(c_ref, w_ref, b_ref, o_ref):
    c = c_ref[...]
    s = (c * jax.nn.sigmoid(c))
    s_hi = s.astype(BF16)
    s_lo = (s - s_hi.astype(F32)).astype(BF16)
    w = w_ref[...].astype(BF16)
    o_ref[...] = _dot(s_hi, w) + _dot(s_lo, w) + b_ref[...]


def ada_params(c_pad, w, b, *, tn=512):
    rows, d = c_pad.shape
    n = w.shape[1]
    return pl.pallas_call(
        _ada_kernel,
        grid=(n // tn,),
        in_specs=[pl.BlockSpec((rows, d), lambda j: (0, 0)),
                  pl.BlockSpec((d, tn), lambda j: (0, j)),
                  pl.BlockSpec((1, tn), lambda j: (0, j))],
        out_specs=pl.BlockSpec((rows, tn), lambda j: (0, j)),
        out_shape=jax.ShapeDtypeStruct((rows, n), F32),
        compiler_params=_params("arbitrary"),
    )(c_pad, w, b)


def _rms(x):
    return x * lax.rsqrt(jnp.mean(x * x, axis=-1, keepdims=True) + EPS)


def _norm_mod_kernel(x_ref, g_ref, mod_ref, o_ref):
    y = _rms(x_ref[...]) * g_ref[...]
    o_ref[...] = (y * (1.0 + mod_ref[0, 1:2, :]) + mod_ref[0, 0:1, :]).astype(o_ref.dtype)


def norm_mod(x, g, mod, seq, *, tm=256, out_dtype=BF16):
    t, d = x.shape
    per_b = seq // tm
    return pl.pallas_call(
        _norm_mod_kernel,
        grid=(t // tm,),
        in_specs=[pl.BlockSpec((tm, d), lambda i: (i, 0)),
                  pl.BlockSpec((1, d), lambda i: (0, 0)),
                  pl.BlockSpec((1, 3, d), lambda i: (i // per_b, 0, 0))],
        out_specs=pl.BlockSpec((tm, d), lambda i: (i, 0)),
        out_shape=jax.ShapeDtypeStruct((t, d), out_dtype),
        compiler_params=_params("arbitrary"),
    )(x, g, mod)


def _norm_router_kernel(x_ref, g_ref, mod_ref, wr_ref, h_ref, logit_ref):
    y = _rms(x_ref[...]) * g_ref[...]
    h = y * (1.0 + mod_ref[0, 1:2, :]) + mod_ref[0, 0:1, :]
    h_ref[...] = h
    logit_ref[...] = jnp.dot(h, wr_ref[...], preferred_element_type=F32, precision=lax.Precision.HIGHEST)


def norm_router(x, g, mod, w_router_pad, seq, *, tm=256):
    t, d = x.shape
    per_b = seq // tm
    return pl.pallas_call(
        _norm_router_kernel,
        grid=(t // tm,),
        in_specs=[pl.BlockSpec((tm, d), lambda i: (i, 0)),
                  pl.BlockSpec((1, d), lambda i: (0, 0)),
                  pl.BlockSpec((1, 3, d), lambda i: (i // per_b, 0, 0)),
                  pl.BlockSpec((d, LANES), lambda i: (0, 0))],
        out_specs=[pl.BlockSpec((tm, d), lambda i: (i, 0)),
                   pl.BlockSpec((tm, LANES), lambda i: (i, 0))],
        out_shape=[jax.ShapeDtypeStruct((t, d), F32), jax.ShapeDtypeStruct((t, LANES), F32)],
        compiler_params=_params("arbitrary"),
    )(x, g, mod, w_router_pad)


def _final_norm_kernel(x_ref, g_ref, o_ref):
    o_ref[...] = _rms(x_ref[...]) * g_ref[...]


def final_norm(x, g, *, tm=256):
    t, d = x.shape
    return pl.pallas_call(
        _final_norm_kernel,
        grid=(t // tm,),
        in_specs=[pl.BlockSpec((tm, d), lambda i: (i, 0)), pl.BlockSpec((1, d), lambda i: (0, 0))],
        out_specs=pl.BlockSpec((tm, d), lambda i: (i, 0)),
        out_shape=jax.ShapeDtypeStruct((t, d), F32),
        compiler_params=_params("arbitrary"),
    )(x, g)


def _cast_once(w_ref, wbf_ref):
    @pl.when(pl.program_id(1) == 0)
    def _():
        wbf_ref[...] = w_ref[...].astype(BF16)


def _mm_kernel(a_ref, w_ref, o_ref, wbf):
    _cast_once(w_ref, wbf)
    o_ref[...] = _dot(a_ref[...], wbf[...]).astype(o_ref.dtype)


def matmul(a, w, *, tm, tn, out_dtype):
    m, k = a.shape
    n = w.shape[1]
    return pl.pallas_call(
        _mm_kernel,
        grid=(n // tn, m // tm),
        in_specs=[pl.BlockSpec((tm, k), lambda j, i: (i, 0)),
                  pl.BlockSpec((k, tn), lambda j, i: (0, j))],
        out_specs=pl.BlockSpec((tm, tn), lambda j, i: (i, j)),
        out_shape=jax.ShapeDtypeStruct((m, n), out_dtype),
        scratch_shapes=[pltpu.VMEM((k, tn), BF16)],
        compiler_params=_params("arbitrary", "arbitrary"),
    )(a, w)


def _mm_resid_kernel(a_ref, w_ref, x_ref, mod_ref, o_ref, wbf):
    _cast_once(w_ref, wbf)
    o_ref[...] = x_ref[...] + mod_ref[0, 2:3, :] * _dot(a_ref[...], wbf[...])


def matmul_resid(a, w, x, mod, seq, *, tm, tn):
    m, k = a.shape
    n = w.shape[1]
    per_b = seq // tm
    return pl.pallas_call(
        _mm_resid_kernel,
        grid=(n // tn, m // tm),
        in_specs=[pl.BlockSpec((tm, k), lambda j, i: (i, 0)),
                  pl.BlockSpec((k, tn), lambda j, i: (0, j)),
                  pl.BlockSpec((tm, tn), lambda j, i: (i, j)),
                  pl.BlockSpec((1, 3, tn), lambda j, i: (i // per_b, 0, j))],
        out_specs=pl.BlockSpec((tm, tn), lambda j, i: (i, j)),
        out_shape=jax.ShapeDtypeStruct((m, n), F32),
        scratch_shapes=[pltpu.VMEM((k, tn), BF16)],
        compiler_params=_params("arbitrary", "arbitrary"),
    )(a, w, x, mod)


def _mm2_resid_kernel(a_ref, b_ref, w_ref, x_ref, mod_ref, o_ref, wbf):
    _cast_once(w_ref, wbf)
    ka = a_ref.shape[1]
    acc = _dot(a_ref[...], wbf[:ka, :]) + _dot(b_ref[...], wbf[ka:, :])
    o_ref[...] = x_ref[...] + mod_ref[0, 2:3, :] * acc


def matmul2_resid(a, b, w, x, mod, seq, *, tm, tn):
    m, ka = a.shape
    kb = b.shape[1]
    n = w.shape[1]
    per_b = seq // tm
    return pl.pallas_call(
        _mm2_resid_kernel,
        grid=(n // tn, m // tm),
        in_specs=[pl.BlockSpec((tm, ka), lambda j, i: (i, 0)),
                  pl.BlockSpec((tm, kb), lambda j, i: (i, 0)),
                  pl.BlockSpec((ka + kb, tn), lambda j, i: (0, j)),
                  pl.BlockSpec((tm, tn), lambda j, i: (i, j)),
                  pl.BlockSpec((1, 3, tn), lambda j, i: (i // per_b, 0, j))],
        out_specs=pl.BlockSpec((tm, tn), lambda j, i: (i, j)),
        out_shape=jax.ShapeDtypeStruct((m, n), F32),
        scratch_shapes=[pltpu.VMEM((ka + kb, tn), BF16)],
        compiler_params=_params("arbitrary", "arbitrary"),
    )(a, b, w, x, mod)


def _cast2_once(w1_ref, w2_ref, w1bf, w2bf):
    @pl.when(pl.program_id(1) == 0)
    def _():
        w1bf[...] = w1_ref[...].astype(BF16)
        w2bf[...] = w2_ref[...].astype(BF16)


def _mm_swiglu_kernel(a_ref, wg_ref, wu_ref, o_ref, wgbf, wubf):
    _cast2_once(wg_ref, wu_ref, wgbf, wubf)
    a = a_ref[...]
    g = _dot(a, wgbf[...])
    u = _dot(a, wubf[...])
    o_ref[...] = (g * jax.nn.sigmoid(g) * u).astype(o_ref.dtype)


def matmul_swiglu(a, wg, wu, *, tm, tn):
    m, k = a.shape
    n = wg.shape[1]
    return pl.pallas_call(
        _mm_swiglu_kernel,
        grid=(n // tn, m // tm),
        in_specs=[pl.BlockSpec((tm, k), lambda j, i: (i, 0)),
                  pl.BlockSpec((k, tn), lambda j, i: (0, j)),
                  pl.BlockSpec((k, tn), lambda j, i: (0, j))],
        out_specs=pl.BlockSpec((tm, tn), lambda j, i: (i, j)),
        out_shape=jax.ShapeDtypeStruct((m, n), BF16),
        scratch_shapes=[pltpu.VMEM((k, tn), BF16), pltpu.VMEM((k, tn), BF16)],
        compiler_params=_params("arbitrary", "arbitrary"),
    )(a, wg, wu)


def _mm_glu_resid_kernel(a_ref, wa_ref, wb_ref, x_ref, mod_ref, o_ref, wabf, wbbf):
    _cast2_once(wa_ref, wb_ref, wabf, wbbf)
    a = a_ref[...]
    p = _dot(a, wabf[...])
    q = _dot(a, wbbf[...])
    o_ref[...] = x_ref[...] + mod_ref[0, 2:3, :] * (p * jax.nn.sigmoid(q))


def matmul_glu_resid(a, wa, wb, x, mod, seq, *, tm, tn):
    m, k = a.shape
    n = wa.shape[1]
    per_b = seq // tm
    return pl.pallas_call(
        _mm_glu_resid_kernel,
        grid=(n // tn, m // tm),
        in_specs=[pl.BlockSpec((tm, k), lambda j, i: (i, 0)),
                  pl.BlockSpec((k, tn), lambda j, i: (0, j)),
                  pl.BlockSpec((k, tn), lambda j, i: (0, j)),
                  pl.BlockSpec((tm, tn), lambda j, i: (i, j)),
                  pl.BlockSpec((1, 3, tn), lambda j, i: (i // per_b, 0, j))],
        out_specs=pl.BlockSpec((tm, tn), lambda j, i: (i, j)),
        out_shape=jax.ShapeDtypeStruct((m, n), F32),
        scratch_shapes=[pltpu.VMEM((k, tn), BF16), pltpu.VMEM((k, tn), BF16)],
        compiler_params=_params("arbitrary", "arbitrary"),
    )(a, wa, wb, x, mod)


def _sb_kernel(q_ref, k_ref, v_ref, o_ref, *, tile, scale):
    qi = pl.program_id(2)
    q = q_ref[...]
    row = lax.broadcasted_iota(jnp.int32, (tile, tile), 0)
    col = lax.broadcasted_iota(jnp.int32, (tile, tile), 1)
    revcum = jnp.where(row >= col, 1.0, 0.0).astype(BF16)

    def body(step, carry):
        acc, later = carry
        kj = qi - step
        start = pl.multiple_of(kj * tile, tile)
        kb = k_ref[pl.ds(start, tile), :]
        vb = v_ref[pl.ds(start, tile), :]
        z = lax.dot_general(q, kb, (((1,), (1,)), ((), ())), preferred_element_type=F32) * scale
        past = (col + kj * tile) < (row + qi * tile)
        sp = jnp.where(past, jnp.maximum(z, 0.0) + jnp.log1p(jnp.exp(-jnp.abs(z))), 0.0)
        sp_hi = sp.astype(BF16)
        sp_lo = (sp - sp_hi.astype(F32)).astype(BF16)
        cum = _dot(sp_hi, revcum) + _dot(sp_lo, revcum)
        w = jnp.where(past, jnp.exp(z - cum - later), 0.0)
        acc = acc + _dot(w.astype(BF16), vb)
        later = later + jnp.sum(sp, axis=1, keepdims=True)
        return acc, later

    acc0 = jnp.zeros((tile, q.shape[1]), F32)
    later0 = jnp.zeros((tile, 1), F32)
    acc, _ = lax.fori_loop(0, qi + 1, body, (acc0, later0))
    o_ref[...] = acc.astype(o_ref.dtype)


def stick_breaking_attention(qkvz, bsz, seq, n_heads, *, tile=SB_TILE):
    t = qkvz.shape[0]
    dh = SB_HEAD_DIM
    nq = seq // tile
    kern = functools.partial(_sb_kernel, tile=tile, scale=1.0 / math.sqrt(dh))
    return pl.pallas_call(
        kern,
        grid=(bsz, n_heads, nq),
        in_specs=[pl.BlockSpec((tile, dh), lambda b, h, i: (b * nq + i, h)),
                  pl.BlockSpec((seq, dh), lambda b, h, i: (b, n_heads + h)),
                  pl.BlockSpec((seq, dh), lambda b, h, i: (b, 2 * n_heads + h))],
        out_specs=pl.BlockSpec((tile, dh), lambda b, h, i: (b * nq + i, h)),
        out_shape=jax.ShapeDtypeStruct((t, n_heads * dh), BF16),
        compiler_params=_params("arbitrary", "arbitrary", "arbitrary"),
    )(qkvz, qkvz, qkvz)


def _gelu(x):
    return jax.nn.gelu(x, approximate=True)


def _gm_kernel(z1_ref, z2_ref, lng_ref, ws_ref, bs_ref, o_ref, *, chunk):
    u = _gelu(z1_ref[...].astype(F32))
    v = _gelu(z2_ref[...].astype(F32))
    mu = jnp.mean(v, axis=-1, keepdims=True)
    vc = v - mu
    var = jnp.mean(vc * vc, axis=-1, keepdims=True)
    vn = vc * lax.rsqrt(var + EPS) * lng_ref[0]
    row = lax.broadcasted_iota(jnp.int32, (chunk, chunk), 0)
    col = lax.broadcasted_iota(jnp.int32, (chunk, chunk), 1)
    w = jnp.where(row >= col, ws_ref[0], 0.0)
    bias = bs_ref[0]
    for ci in range(u.shape[0] // chunk):
        sl = slice(ci * chunk, (ci + 1) * chunk)
        mixed = jnp.dot(w, vn[sl], preferred_element_type=F32, precision=lax.Precision.HIGHEST) + bias
        o_ref[sl, :] = (u[sl] * mixed).astype(o_ref.dtype)


def spatial_gating(qkvz, ln_g, w_s, b_s, n_groups, *, rows=GM_ROWS):
    t = qkvz.shape[0]
    c = GM_GROUP_DIM
    kern = functools.partial(_gm_kernel, chunk=GM_CHUNK)
    return pl.pallas_call(
        kern,
        grid=(n_groups, t // rows),
        in_specs=[pl.BlockSpec((rows, c), lambda g, i: (i, 3 * n_groups + g)),
                  pl.BlockSpec((rows, c), lambda g, i: (i, 4 * n_groups + g)),
                  pl.BlockSpec((1, 1, c), lambda g, i: (g, 0, 0)),
                  pl.BlockSpec((1, GM_CHUNK, GM_CHUNK), lambda g, i: (g, 0, 0)),
                  pl.BlockSpec((1, GM_CHUNK, 1), lambda g, i: (g, 0, 0))],
        out_specs=pl.BlockSpec((rows, c), lambda g, i: (i, g)),
        out_shape=jax.ShapeDtypeStruct((t, n_groups * c), BF16),
        compiler_params=_params("arbitrary", "arbitrary"),
    )(qkvz, qkvz, ln_g[:, None, :], w_s, b_s[:, :, None])


def ssm_operators(lam_re, lam_im, log_dt, b_re, b_im, c_re, c_im):
    g_all, p = lam_re.shape
    cg = SSM_GROUP
    L = SSM_CHUNK
    gpt = LANES // cg
    nt = g_all // gpt
    lr = jnp.minimum(lam_re, -1e-4)
    li = lam_im
    dt = jnp.exp(log_dt)[:, None]
    mag = jnp.exp(lr * dt)
    a_re = mag * jnp.cos(li * dt)
    a_im = mag * jnp.sin(li * dt)
    den = lr * lr + li * li
    nr = a_re - 1.0
    f_re = (nr * lr + a_im * li) / den
    f_im = (a_im * lr - nr * li) / den
    bb_re = f_re[:, :, None] * b_re - f_im[:, :, None] * b_im
    bb_im = f_re[:, :, None] * b_im + f_im[:, :, None] * b_re
    tau = jnp.arange(L + 1, dtype=F32)[:, None, None]
    pmag = jnp.exp(lr * dt * tau)
    pw_re = pmag * jnp.cos(li * dt * tau)
    pw_im = pmag * jnp.sin(li * dt * tau)
    ab_re = pw_re[..., None] * bb_re - pw_im[..., None] * bb_im
    ab_im = pw_re[..., None] * bb_im + pw_im[..., None] * bb_re
    kk = (jnp.einsum('gcp,lgpd->lgcd', c_re, ab_re[:L], precision=lax.Precision.HIGHEST)
          - jnp.einsum('gcp,lgpd->lgcd', c_im, ab_im[:L], precision=lax.Precision.HIGHEST))
    eye = jnp.eye(gpt, dtype=F32)
    s_in = jnp.arange(L)[:, None]
    s_out = jnp.arange(L)[None, :]
    lag = s_out - s_in
    toe = jnp.where((lag >= 0)[:, :, None, None, None], kk[jnp.clip(lag, 0, L - 1)], 0.0)
    toe = toe.reshape(L, L, nt, gpt, cg, cg)
    m_op = jnp.einsum('abjgcd,gh->jagdbhc', toe, eye).reshape(nt, L * LANES, L * LANES)
    st_re = ab_re[:L][::-1].reshape(L, nt, gpt, p, cg)
    st_im = ab_im[:L][::-1].reshape(L, nt, gpt, p, cg)
    b_op = jnp.stack([jnp.einsum('ajgpd,gh->jagdhp', st_re, eye),
                      jnp.einsum('ajgpd,gh->jagdhp', st_im, eye)], axis=4)
    b_op = b_op.reshape(nt, L * LANES, 2 * gpt * p)
    q_re = pw_re[1:]
    q_im = pw_im[1:]
    co_re = c_re[None] * q_re[:, :, None, :] - c_im[None] * q_im[:, :, None, :]
    co_im = -(c_re[None] * q_im[:, :, None, :] + c_im[None] * q_re[:, :, None, :])
    co_re = co_re.reshape(L, nt, gpt, cg, p)
    co_im = co_im.reshape(L, nt, gpt, cg, p)
    c_op = jnp.stack([jnp.einsum('bjhcp,gh->jgpbhc', co_re, eye),
                      jnp.einsum('bjhcp,gh->jgpbhc', co_im, eye)], axis=1)
    c_op = c_op.reshape(nt, 2 * gpt * p, L * LANES)
    al_re = pw_re[L].reshape(nt, 1, gpt * p)
    al_im = pw_im[L].reshape(nt, 1, gpt * p)
    return m_op.astype(BF16), b_op.astype(BF16), c_op.astype(BF16), al_re, al_im


def _ssm_kernel(u_ref, m_ref, b_ref, c_ref, are_ref, aim_ref, d_ref, o_ref, xl3, xp3, pw3, *, bsz, nseg, seglen):
    L = u_ref.shape[1]
    nblk = xl3.shape[0]
    hb = nblk // 2
    nseq = bsz * nseg
    u2 = jnp.concatenate([u_ref[:, s, :] for s in range(L)], axis=1).astype(BF16)
    xl = _dot(u2, b_ref[0])
    for kb in range(nblk):
        xl3[kb] = xl[:, kb * LANES:(kb + 1) * LANES]
    a_re = [are_ref[0, :, kb * LANES:(kb + 1) * LANES] for kb in range(hb)]
    a_im = [aim_ref[0, :, kb * LANES:(kb + 1) * LANES] for kb in range(hb)]

    def step(k, carry):
        xs, ps = carry
        rows = pl.ds(k, nseq, stride=seglen)
        new_x, new_p = [], []
        for kb in range(hb):
            xr, xi = xs[kb]
            pr, pi = ps[kb]
            xp3.at[kb][rows, :] = xr
            xp3.at[hb + kb][rows, :] = xi
            pw3.at[kb][pl.ds(k, 1), :] = pr
            pw3.at[hb + kb][pl.ds(k, 1), :] = pi
            lr = xl3.at[kb][rows, :]
            li = xl3.at[hb + kb][rows, :]
            new_x.append((a_re[kb] * xr - a_im[kb] * xi + lr, a_re[kb] * xi + a_im[kb] * xr + li))
            new_p.append((a_re[kb] * pr - a_im[kb] * pi, a_re[kb] * pi + a_im[kb] * pr))
        return tuple(new_x), tuple(new_p)

    x0 = tuple((jnp.zeros((nseq, LANES), F32), jnp.zeros((nseq, LANES), F32)) for _ in range(hb))
    p0 = tuple((jnp.ones((1, LANES), F32), jnp.zeros((1, LANES), F32)) for _ in range(hb))
    xs, ps = lax.fori_loop(0, seglen, step, (x0, p0))

    for kb in range(hb):
        pr = pw3[kb]
        pi = pw3[hb + kb]
        fr, fi = xs[kb]
        qr, qi = ps[kb]
        for b in range(bsz):
            sr = si = None
            for q in range(1, nseg):
                m = b * nseg + q
                er, ei = fr[m - 1:m], fi[m - 1:m]
                if sr is None:
                    sr, si = er, ei
                else:
                    sr, si = er + qr * sr - qi * si, ei + qr * si + qi * sr
                sl = slice(m * seglen, (m + 1) * seglen)
                xp3[kb, sl, :] = xp3[kb, sl, :] + (pr * sr - pi * si)
                xp3[hb + kb, sl, :] = xp3[hb + kb, sl, :] + (pr * si + pi * sr)

    xp = jnp.concatenate([xp3[kb] for kb in range(nblk)], axis=1).astype(BF16)
    y = _dot(u2, m_ref[0]) + _dot(xp, c_ref[0])
    d = d_ref[...]
    for s in range(L):
        ys = y[:, s * LANES:(s + 1) * LANES] + d * u_ref[:, s, :]
        o_ref[:, s, :] = _gelu(ys)


def ssm_mixer(u3, ops, d_skip, bsz):
    m_op, b_op, c_op, al_re, al_im = ops
    nrow, L, width = u3.shape
    nt = width // LANES
    nstate = b_op.shape[2]
    sublanes = 8
    nseg = sublanes // bsz
    seglen = nrow // (bsz * nseg)
    nblk = nstate // LANES
    kern = functools.partial(_ssm_kernel, bsz=bsz, nseg=nseg, seglen=seglen)
    return pl.pallas_call(
        kern,
        grid=(nt,),
        in_specs=[pl.BlockSpec((nrow, L, LANES), lambda j: (0, 0, j)),
                  pl.BlockSpec((1, L * LANES, L * LANES), lambda j: (j, 0, 0)),
                  pl.BlockSpec((1, L * LANES, nstate), lambda j: (j, 0, 0)),
                  pl.BlockSpec((1, nstate, L * LANES), lambda j: (j, 0, 0)),
                  pl.BlockSpec((1, 1, nstate // 2), lambda j: (j, 0, 0)),
                  pl.BlockSpec((1, 1, nstate // 2), lambda j: (j, 0, 0)),
                  pl.BlockSpec((1, LANES), lambda j: (0, j))],
        out_specs=pl.BlockSpec((nrow, L, LANES), lambda j: (0, 0, j)),
        out_shape=jax.ShapeDtypeStruct((nrow, L, width), F32),
        scratch_shapes=[pltpu.VMEM((nblk, nrow, LANES), F32), pltpu.VMEM((nblk, nrow, LANES), F32),
                        pltpu.VMEM((nblk, seglen, LANES), F32)],
        compiler_params=_params("arbitrary"),
        name="ssm_mixer",
    )(u3, m_op, b_op, c_op, al_re, al_im, d_skip)


def _route_kernel(logit_ref, idx_ref, gate_ref, *, n_experts):
    lg = logit_ref[...]
    lane = lax.broadcasted_iota(jnp.int32, lg.shape, 1)
    neg = jnp.float32(-jnp.inf)
    lg = jnp.where(lane < n_experts, lg, neg)
    m1 = jnp.max(lg, axis=-1, keepdims=True)
    i1 = jnp.min(jnp.where(lg == m1, lane, LANES), axis=-1, keepdims=True)
    lg2 = jnp.where(lane == i1, neg, lg)
    m2 = jnp.max(lg2, axis=-1, keepdims=True)
    i2 = jnp.min(jnp.where(lg2 == m2, lane, LANES), axis=-1, keepdims=True)
    e2 = jnp.exp(m2 - m1)
    p1 = 1.0 / (1.0 + e2)
    p2 = e2 / (1.0 + e2)
    idx_ref[...] = jnp.where(lane == 0, i1, jnp.where(lane == 1, i2, 0))
    gate_ref[...] = jnp.where(lane == 0, p1, jnp.where(lane == 1, p2, 0.0))


def route_top2(logits, *, tm=512):
    t = logits.shape[0]
    kern = functools.partial(_route_kernel, n_experts=N_EXPERTS)
    return pl.pallas_call(
        kern,
        grid=(t // tm,),
        in_specs=[pl.BlockSpec((tm, LANES), lambda i: (i, 0))],
        out_specs=[pl.BlockSpec((tm, LANES), lambda i: (i, 0)), pl.BlockSpec((tm, LANES), lambda i: (i, 0))],
        out_shape=[jax.ShapeDtypeStruct((t, LANES), jnp.int32), jax.ShapeDtypeStruct((t, LANES), F32)],
        compiler_params=_params("arbitrary"),
    )(logits)


def _dispatch_kernel(tok_ref, h_hbm, o_ref, buf, sem, *, tile):
    base = pl.program_id(0) * tile

    def row_copy(r):
        return pltpu.make_async_copy(h_hbm.at[pl.ds(tok_ref[base + r], 1)], buf.at[pl.ds(r, 1)], sem)

    def issue(r, c):
        row_copy(r).start()
        return c

    lax.fori_loop(0, tile, issue, 0)

    def drain(r, c):
        row_copy(r).wait()
        return c

    lax.fori_loop(0, tile, drain, 0)
    o_ref[...] = buf[...].astype(o_ref.dtype)


def moe_dispatch(h, tok_of_slot, n_slots, *, tile=MOE_TILE):
    d = h.shape[1]
    kern = functools.partial(_dispatch_kernel, tile=tile)
    return pl.pallas_call(
        kern,
        grid_spec=pltpu.PrefetchScalarGridSpec(
            num_scalar_prefetch=1,
            grid=(n_slots // tile,),
            in_specs=[pl.BlockSpec(memory_space=pl.ANY)],
            out_specs=pl.BlockSpec((tile, d), lambda i, tok: (i, 0)),
            scratch_shapes=[pltpu.VMEM((tile, d), F32), pltpu.SemaphoreType.DMA(())]),
        out_shape=jax.ShapeDtypeStruct((n_slots, d), BF16),
        compiler_params=_params("arbitrary"),
    )(tok_of_slot, h)


def _expert_cast(tile_expert_ref, refs, scratches):
    i = pl.program_id(1)
    prev = tile_expert_ref[jnp.maximum(i - 1, 0)]
    changed = jnp.logical_or(i == 0, tile_expert_ref[i] != prev)

    @pl.when(changed)
    def _():
        for r, s in zip(refs, scratches):
            s[...] = r[0].astype(BF16)


def _moe_up_kernel(te_ref, na_ref, a_ref, wg_ref, wu_ref, o_ref, wgbf, wubf):
    _expert_cast(te_ref, (wg_ref, wu_ref), (wgbf, wubf))

    @pl.when(pl.program_id(1) < na_ref[0])
    def _():
        a = a_ref[...]
        g = _dot(a, wgbf[...])
        u = _dot(a, wubf[...])
        o_ref[...] = (g * jax.nn.sigmoid(g) * u).astype(o_ref.dtype)

    @pl.when(pl.program_id(1) >= na_ref[0])
    def _():
        o_ref[...] = jnp.zeros_like(o_ref)


def moe_up(hg, tile_expert, n_active, wg, wu, *, tile=MOE_TILE, tn=256):
    r, k = hg.shape
    n = wg.shape[2]
    return pl.pallas_call(
        _moe_up_kernel,
        grid_spec=pltpu.PrefetchScalarGridSpec(
            num_scalar_prefetch=2,
            grid=(n // tn, r // tile),
            in_specs=[pl.BlockSpec((tile, k), lambda j, i, te, na: (i, 0)),
                      pl.BlockSpec((1, k, tn), lambda j, i, te, na: (te[i], 0, j)),
                      pl.BlockSpec((1, k, tn), lambda j, i, te, na: (te[i], 0, j))],
            out_specs=pl.BlockSpec((tile, tn), lambda j, i, te, na: (i, j)),
            scratch_shapes=[pltpu.VMEM((k, tn), BF16), pltpu.VMEM((k, tn), BF16)]),
        out_shape=jax.ShapeDtypeStruct((r, n), BF16),
        compiler_params=_params("arbitrary", "arbitrary"),
    )(tile_expert, n_active, hg, wg, wu)


def _moe_down_kernel(te_ref, na_ref, a_ref, w_ref, o_ref, wbf):
    _expert_cast(te_ref, (w_ref,), (wbf,))

    @pl.when(pl.program_id(1) < na_ref[0])
    def _():
        o_ref[...] = _dot(a_ref[...], wbf[...])

    @pl.when(pl.program_id(1) >= na_ref[0])
    def _():
        o_ref[...] = jnp.zeros_like(o_ref)


def moe_down(act, tile_expert, n_active, wd, *, tile=MOE_TILE, tn=512):
    r, k = act.shape
    n = wd.shape[2]
    return pl.pallas_call(
        _moe_down_kernel,
        grid_spec=pltpu.PrefetchScalarGridSpec(
            num_scalar_prefetch=2,
            grid=(n // tn, r // tile),
            in_specs=[pl.BlockSpec((tile, k), lambda j, i, te, na: (i, 0)),
                      pl.BlockSpec((1, k, tn), lambda j, i, te, na: (te[i], 0, j))],
            out_specs=pl.BlockSpec((tile, tn), lambda j, i, te, na: (i, j)),
            scratch_shapes=[pltpu.VMEM((k, tn), BF16)]),
        out_shape=jax.ShapeDtypeStruct((r, n), F32),
        compiler_params=_params("arbitrary", "arbitrary"),
    )(tile_expert, n_active, act, wd)


def _combine_kernel(pos_ref, og_hbm, x_ref, gate_ref, mod_ref, fg_ref, o_ref, buf1, buf2, sem, *, tm):
    base = pl.program_id(0) * tm

    def copies(r):
        t = base + r
        return (pltpu.make_async_copy(og_hbm.at[pl.ds(pos_ref[2 * t], 1)], buf1.at[pl.ds(r, 1)], sem),
                pltpu.make_async_copy(og_hbm.at[pl.ds(pos_ref[2 * t + 1], 1)], buf2.at[pl.ds(r, 1)], sem))

    def issue(r, c):
        c1, c2 = copies(r)
        c1.start()
        c2.start()
        return c

    lax.fori_loop(0, tm, issue, 0)

    def drain(r, c):
        c1, c2 = copies(r)
        c1.wait()
        c2.wait()
        return c

    lax.fori_loop(0, tm, drain, 0)
    g = gate_ref[...]
    moe = g[:, 0:1] * buf1[...] + g[:, 1:2] * buf2[...]
    xn = x_ref[...] + mod_ref[0, 2:3, :] * moe
    o_ref[...] = _rms(xn) * fg_ref[...]


def moe_combine_final(og, pos_flat, x, gates, mod, final_g, seq, *, tm=128):
    t, d = x.shape
    per_b = seq // tm
    kern = functools.partial(_combine_kernel, tm=tm)
    return pl.pallas_call(
        kern,
        grid_spec=pltpu.PrefetchScalarGridSpec(
            num_scalar_prefetch=1,
            grid=(t // tm,),
            in_specs=[pl.BlockSpec(memory_space=pl.ANY),
                      pl.BlockSpec((tm, d), lambda i, pos: (i, 0)),
                      pl.BlockSpec((tm, LANES), lambda i, pos: (i, 0)),
                      pl.BlockSpec((1, 3, d), lambda i, pos: (i // per_b, 0, 0)),
                      pl.BlockSpec((1, d), lambda i, pos: (0, 0))],
            out_specs=pl.BlockSpec((tm, d), lambda i, pos: (i, 0)),
            scratch_shapes=[pltpu.VMEM((tm, d), F32), pltpu.VMEM((tm, d), F32), pltpu.SemaphoreType.DMA(())]),
        out_shape=jax.ShapeDtypeStruct((t, d), F32),
        compiler_params=_params("arbitrary"),
    )(pos_flat, og, x, gates, mod, final_g)


def moe_plan(idx, n_tokens, *, tile=MOE_TILE):
    e1 = idx[:, 0]
    e2 = idx[:, 1]
    flat_e = jnp.stack([e1, e2], axis=1).reshape(-1)
    onehot = (flat_e[:, None] == jnp.arange(N_EXPERTS)[None, :]).astype(jnp.int32)
    rank = jnp.cumsum(onehot, axis=0) - onehot
    counts = jnp.sum(onehot, axis=0)
    padded = ((counts + tile - 1) // tile) * tile
    ends = jnp.cumsum(padded)
    starts = ends - padded
    pos = jnp.sum(onehot * (rank + starts[None, :]), axis=1)
    n_slots = 2 * n_tokens + N_EXPERTS * tile
    n_tiles = n_slots // tile
    tok = jnp.repeat(jnp.arange(n_tokens, dtype=jnp.int32), 2)
    tok_of_slot = jnp.zeros((n_slots,), jnp.int32).at[pos].set(tok)
    n_active = (ends[-1] // tile).astype(jnp.int32)
    tile_start = jnp.arange(n_tiles, dtype=jnp.int32) * tile
    tile_expert = jnp.sum((tile_start[:, None] >= ends[None, :]).astype(jnp.int32), axis=1)
    last_expert = tile_expert[jnp.maximum(n_active - 1, 0)]
    tile_expert = jnp.where(jnp.arange(n_tiles) < n_active, tile_expert, last_expert).astype(jnp.int32)
    return pos.astype(jnp.int32), tok_of_slot, tile_expert, n_active.reshape(1), n_slots


def kernel(x, c, mix0_norm_g, mix0_ada_w, mix0_ada_b, mix0_w_in, gm_ln_g, gm_w_s, gm_b_s, mix0_w_out, ffn0_norm_g, ffn0_ada_w, ffn0_ada_b, ffn0_w_gate, ffn0_w_up, ffn0_w_down, mix1_norm_g, mix1_ada_w, mix1_ada_b, ssm_w_in, ssm_lam_re, ssm_lam_im, ssm_log_dt, ssm_b_re, ssm_b_im, ssm_c_re, ssm_c_im, ssm_d, glu_w_a, glu_w_b, moe_norm_g, moe_ada_w, moe_ada_b, moe_w_router, moe_w_gate, moe_w_up, moe_w_down, final_norm_g):
    bsz, seq, d = x.shape
    t = bsz * seq
    n_heads = mix0_w_in.shape[2] // (5 * SB_HEAD_DIM)
    n_groups = gm_w_s.shape[1]
    xf = x.reshape(t, d)
    c_pad = jnp.zeros((8, d), F32).at[:bsz].set(c)
    tm_big = min(1024, seq)

    def ada(w, b):
        m = ada_params(c_pad, w[0], b[0][None, :])
        return m[:bsz].reshape(bsz, 3, d)

    mod = ada(mix0_ada_w, mix0_ada_b)
    h = norm_mod(xf, mix0_norm_g, mod, seq)
    qkvz = matmul(h, mix0_w_in[0], tm=tm_big,tn=512, out_dtype=BF16)
    a_out = stick_breaking_attention(qkvz, bsz, seq, n_heads)
    b_out = spatial_gating(qkvz, gm_ln_g[0], gm_w_s[0], gm_b_s[0], n_groups)
    xf = matmul2_resid(a_out, b_out, mix0_w_out[0], xf, mod, seq, tm=tm_big,tn=512)
    mod = ada(ffn0_ada_w, ffn0_ada_b)
    h = norm_mod(xf, ffn0_norm_g, mod, seq)
    act = matmul_swiglu(h, ffn0_w_gate[0], ffn0_w_up[0], tm=tm_big,tn=256)
    xf = matmul_resid(act, ffn0_w_down[0], xf, mod, seq, tm=256, tn=256)
    mod = ada(mix1_ada_w, mix1_ada_b)
    h = norm_mod(xf, mix1_norm_g, mod, seq)
    u = matmul(h, ssm_w_in[0], tm=tm_big,tn=512, out_dtype=F32)
    ops = ssm_operators(ssm_lam_re[0], ssm_lam_im[0], ssm_log_dt[0], ssm_b_re[0], ssm_b_im[0],
                        ssm_c_re[0], ssm_c_im[0])
    y3 = ssm_mixer(u.reshape(t // SSM_CHUNK, SSM_CHUNK, d), ops, ssm_d, bsz)
    y = y3.reshape(t, d).astype(BF16)
    xf = matmul_glu_resid(y, glu_w_a[0], glu_w_b[0], xf, mod, seq, tm=tm_big,tn=256)
    mod = ada(moe_ada_w, moe_ada_b)
    w_router_pad = jnp.zeros((d, LANES), F32).at[:, :N_EXPERTS].set(moe_w_router[0])
    h32, logits = norm_router(xf, moe_norm_g, mod, w_router_pad, seq)
    idx, gates = route_top2(logits)
    pos, tok_of_slot, tile_expert, n_active, n_slots = moe_plan(idx, t)
    hg = moe_dispatch(h32, tok_of_slot, n_slots)
    act = moe_up(hg, tile_expert, n_active, moe_w_gate[0], moe_w_up[0])
    og = moe_down(act, tile_expert, n_active, moe_w_down[0])
    out = moe_combine_final(og, pos, xf, gates, mod, final_norm_g[None, :], seq)
    return out.reshape(bsz, seq, d)
```

```python
import functools
import math

import jax
import jax.numpy as jnp
from jax import lax
from jax.experimental import pallas as pl
from jax.experimental.pallas import tpu as pltpu

EPS = 1e-6
SB_HEAD_DIM = 128
SB_TILE = 256
GM_GROUP_DIM = 128
GM_CHUNK = 128
GM_ROWS = 512
SSM_GROUP = 16
SSM_STATE = 64
SSM_CHUNK = 8
LANES = 128
N_EXPERTS = 8
MOE_TILE = 256
SB_SKIP_EXPONENT = 104.0
VMEM_LIMIT_BYTES = 56 * 1024 * 1024

BF16 = jnp.bfloat16
F32 = jnp.float32


def _params(*sem):
    return pltpu.CompilerParams(dimension_semantics=sem, vmem_limit_bytes=VMEM_LIMIT_BYTES)


def _dot(a, b):
    return jnp.dot(a, b, preferred_element_type=F32)


def _ada_kernel(c_ref, w_ref, b_ref, o_ref):
    c = c_ref[...]
    s = (c * jax.nn.sigmoid(c))
    s_hi = s.astype(BF16)
    s_lo = (s - s_hi.astype(F32)).astype(BF16)
    w = w_ref[...].astype(BF16)
    o_ref[...] = _dot(s_hi, w) + _dot(s_lo, w) + b_ref[...]


def ada_params(c_pad, w, b, *, tn=512):
    rows, d = c_pad.shape
    n = w.shape[1]
    return pl.pallas_call(
        _ada_kernel,
        grid=(n // tn,),
        in_specs=[pl.BlockSpec((rows, d), lambda j: (0, 0)),
                  pl.BlockSpec((d, tn), lambda j: (0, j)),
                  pl.BlockSpec((1, tn), lambda j: (0, j))],
        out_specs=pl.BlockSpec((rows, tn), lambda j: (0, j)),
        out_shape=jax.ShapeDtypeStruct((rows, n), F32),
        compiler_params=_params("arbitrary"),
    )(c_pad, w, b)


def _rms(x):
    return x * lax.rsqrt(jnp.mean(x * x, axis=-1, keepdims=True) + EPS)


def _norm_mod_kernel(x_ref, g_ref, mod_ref, o_ref):
    y = _rms(x_ref[...]) * g_ref[...]
    o_ref[...] = (y * (1.0 + mod_ref[0, 1:2, :]) + mod_ref[0, 0:1, :]).astype(o_ref.dtype)


def norm_mod(x, g, mod, seq, *, tm=256, out_dtype=BF16):
    t, d = x.shape
    per_b = seq // tm
    return pl.pallas_call(
        _norm_mod_kernel,
        grid=(t // tm,),
        in_specs=[pl.BlockSpec((tm, d), lambda i: (i, 0)),
                  pl.BlockSpec((1, d), lambda i: (0, 0)),
                  pl.BlockSpec((1, 3, d), lambda i: (i // per_b, 0, 0))],
        out_specs=pl.BlockSpec((tm, d), lambda i: (i, 0)),
        out_shape=jax.ShapeDtypeStruct((t, d), out_dtype),
        compiler_params=_params("arbitrary"),
    )(x, g, mod)


def _norm_router_kernel(x_ref, g_ref, mod_ref, wr_ref, h_ref, logit_ref):
    y = _rms(x_ref[...]) * g_ref[...]
    h = y * (1.0 + mod_ref[0, 1:2, :]) + mod_ref[0, 0:1, :]
    h_ref[...] = h
    logit_ref[...] = jnp.dot(h, wr_ref[...], preferred_element_type=F32, precision=lax.Precision.HIGHEST)


def norm_router(x, g, mod, w_router_pad, seq, *, tm=256):
    t, d = x.shape
    per_b = seq // tm
    return pl.pallas_call(
        _norm_router_kernel,
        grid=(t // tm,),
        in_specs=[pl.BlockSpec((tm, d), lambda i: (i, 0)),
                  pl.BlockSpec((1, d), lambda i: (0, 0)),
                  pl.BlockSpec((1, 3, d), lambda i: (i // per_b, 0, 0)),
                  pl.BlockSpec((d, LANES), lambda i: (0, 0))],
        out_specs=[pl.BlockSpec((tm, d), lambda i: (i, 0)),
                   pl.BlockSpec((tm, LANES), lambda i: (i, 0))],
        out_shape=[jax.ShapeDtypeStruct((t, d), F32), jax.ShapeDtypeStruct((t, LANES), F32)],
        compiler_params=_params("arbitrary"),
    )(x, g, mod, w_router_pad)


def _final_norm_kernel(x_ref, g_ref, o_ref):
    o_ref[...] = _rms(x_ref[...]) * g_ref[...]


def final_norm(x, g, *, tm=256):
    t, d = x.shape
    return pl.pallas_call(
        _final_norm_kernel,
        grid=(t // tm,),
        in_specs=[pl.BlockSpec((tm, d), lambda i: (i, 0)), pl.BlockSpec((1, d), lambda i: (0, 0))],
        out_specs=pl.BlockSpec((tm, d), lambda i: (i, 0)),
        out_shape=jax.ShapeDtypeStruct((t, d), F32),
        compiler_params=_params("arbitrary"),
    )(x, g)


def _cast_once(w_ref, wbf_ref):
    @pl.when(pl.program_id(1) == 0)
    def _():
        wbf_ref[...] = w_ref[...].astype(BF16)


def _mm_kernel(a_ref, w_ref, o_ref, wbf):
    _cast_once(w_ref, wbf)
    o_ref[...] = _dot(a_ref[...], wbf[...]).astype(o_ref.dtype)


def matmul(a, w, *, tm, tn, out_dtype):
    m, k = a.shape
    n = w.shape[1]
    return pl.pallas_call(
        _mm_kernel,
        grid=(n // tn, m // tm),
        in_specs=[pl.BlockSpec((tm, k), lambda j, i: (i, 0)),
                  pl.BlockSpec((k, tn), lambda j, i: (0, j))],
        out_specs=pl.BlockSpec((tm, tn), lambda j, i: (i, j)),
        out_shape=jax.ShapeDtypeStruct((m, n), out_dtype),
        scratch_shapes=[pltpu.VMEM((k, tn), BF16)],
        compiler_params=_params("arbitrary", "arbitrary"),
    )(a, w)


def _mm_resid_kernel(a_ref, w_ref, x_ref, mod_ref, o_ref, wbf):
    _cast_once(w_ref, wbf)
    o_ref[...] = x_ref[...] + mod_ref[0, 2:3, :] * _dot(a_ref[...], wbf[...])


def matmul_resid(a, w, x, mod, seq, *, tm, tn):
    m, k = a.shape
    n = w.shape[1]
    per_b = seq // tm
    return pl.pallas_call(
        _mm_resid_kernel,
        grid=(n // tn, m // tm),
        in_specs=[pl.BlockSpec((tm, k), lambda j, i: (i, 0)),
                  pl.BlockSpec((k, tn), lambda j, i: (0, j), pipeline_mode=pl.Buffered(1)),
                  pl.BlockSpec((tm, tn), lambda j, i: (i, j)),
                  pl.BlockSpec((1, 3, tn), lambda j, i: (i // per_b, 0, j))],
        out_specs=pl.BlockSpec((tm, tn), lambda j, i: (i, j)),
        out_shape=jax.ShapeDtypeStruct((m, n), F32),
        scratch_shapes=[pltpu.VMEM((k, tn), BF16)],
        compiler_params=_params("arbitrary", "arbitrary"),
    )(a, w, x, mod)


def _mm2_resid_kernel(a_ref, b_ref, w_ref, x_ref, mod_ref, o_ref, wbf):
    _cast_once(w_ref, wbf)
    ka = a_ref.shape[1]
    acc = _dot(a_ref[...], wbf[:ka, :]) + _dot(b_ref[...], wbf[ka:, :])
    o_ref[...] = x_ref[...] + mod_ref[0, 2:3, :] * acc


def matmul2_resid(a, b, w, x, mod, seq, *, tm, tn):
    m, ka = a.shape
    kb = b.shape[1]
    n = w.shape[1]
    per_b = seq // tm
    return pl.pallas_call(
        _mm2_resid_kernel,
        grid=(n // tn, m // tm),
        in_specs=[pl.BlockSpec((tm, ka), lambda j, i: (i, 0)),
                  pl.BlockSpec((tm, kb), lambda j, i: (i, 0)),
                  pl.BlockSpec((ka + kb, tn), lambda j, i: (0, j)),
                  pl.BlockSpec((tm, tn), lambda j, i: (i, j)),
                  pl.BlockSpec((1, 3, tn), lambda j, i: (i // per_b, 0, j))],
        out_specs=pl.BlockSpec((tm, tn), lambda j, i: (i, j)),
        out_shape=jax.ShapeDtypeStruct((m, n), F32),
        scratch_shapes=[pltpu.VMEM((ka + kb, tn), BF16)],
        compiler_params=_params("arbitrary", "arbitrary"),
    )(a, b, w, x, mod)


def _cast2_once(w1_ref, w2_ref, w1bf, w2bf):
    @pl.when(pl.program_id(1) == 0)
    def _():
        w1bf[...] = w1_ref[...].astype(BF16)
        w2bf[...] = w2_ref[...].astype(BF16)


def _mm_swiglu_kernel(a_ref, wg_ref, wu_ref, o_ref, wgbf, wubf):
    _cast2_once(wg_ref, wu_ref, wgbf, wubf)
    a = a_ref[...]
    g = _dot(a, wgbf[...])
    u = _dot(a, wubf[...])
    o_ref[...] = (g * jax.nn.sigmoid(g) * u).astype(o_ref.dtype)


def matmul_swiglu(a, wg, wu, *, tm, tn):
    m, k = a.shape
    n = wg.shape[1]
    return pl.pallas_call(
        _mm_swiglu_kernel,
        grid=(n // tn, m // tm),
        in_specs=[pl.BlockSpec((tm, k), lambda j, i: (i, 0)),
                  pl.BlockSpec((k, tn), lambda j, i: (0, j)),
                  pl.BlockSpec((k, tn), lambda j, i: (0, j))],
        out_specs=pl.BlockSpec((tm, tn), lambda j, i: (i, j)),
        out_shape=jax.ShapeDtypeStruct((m, n), BF16),
        scratch_shapes=[pltpu.VMEM((k, tn), BF16), pltpu.VMEM((k, tn), BF16)],
        compiler_params=_params("arbitrary", "arbitrary"),
    )(a, wg, wu)


def _mm_glu_resid_kernel(a_ref, wa_ref, wb_ref, x_ref, mod_ref, o_ref, wabf, wbbf):
    _cast2_once(wa_ref, wb_ref, wabf, wbbf)
    a = a_ref[...]
    p = _dot(a, wabf[...])
    q = _dot(a, wbbf[...])
    o_ref[...] = x_ref[...] + mod_ref[0, 2:3, :] * (p * jax.nn.sigmoid(q))


def matmul_glu_resid(a, wa, wb, x, mod, seq, *, tm, tn):
    m, k = a.shape
    n = wa.shape[1]
    per_b = seq // tm
    return pl.pallas_call(
        _mm_glu_resid_kernel,
        grid=(n // tn, m // tm),
        in_specs=[pl.BlockSpec((tm, k), lambda j, i: (i, 0)),
                  pl.BlockSpec((k, tn), lambda j, i: (0, j)),
                  pl.BlockSpec((k, tn), lambda j, i: (0, j)),
                  pl.BlockSpec((tm, tn), lambda j, i: (i, j)),
                  pl.BlockSpec((1, 3, tn), lambda j, i: (i // per_b, 0, j))],
        out_specs=pl.BlockSpec((tm, tn), lambda j, i: (i, j)),
        out_shape=jax.ShapeDtypeStruct((m, n), F32),
        scratch_shapes=[pltpu.VMEM((k, tn), BF16), pltpu.VMEM((k, tn), BF16)],
        compiler_params=_params("arbitrary", "arbitrary"),
    )(a, wa, wb, x, mod)


def _sb_kernel(q_ref, k_ref, v_ref, o_ref, *, tile, scale):
    qi = pl.program_id(2)
    q = q_ref[...]
    row = lax.broadcasted_iota(jnp.int32, (tile, tile), 0)
    col = lax.broadcasted_iota(jnp.int32, (tile, tile), 1)
    revcum = jnp.where(row >= col, 1.0, 0.0).astype(BF16)

    def cond(carry):
        step, _, _, min_later = carry
        return jnp.logical_and(step <= qi, min_later <= SB_SKIP_EXPONENT)

    def body(carry):
        step, acc, later, _ = carry
        kj = qi - step
        start = pl.multiple_of(kj * tile, tile)
        kb = k_ref[pl.ds(start, tile), :]
        vb = v_ref[pl.ds(start, tile), :]
        z = lax.dot_general(q, kb, (((1,), (1,)), ((), ())), preferred_element_type=F32) * scale
        past = (col + kj * tile) < (row + qi * tile)
        sp = jnp.where(past, jnp.maximum(z, 0.0) + jnp.log1p(jnp.exp(-jnp.abs(z))), 0.0)
        sp_hi = sp.astype(BF16)
        sp_lo = (sp - sp_hi.astype(F32)).astype(BF16)
        cum = _dot(sp_hi, revcum) + _dot(sp_lo, revcum)
        w = jnp.where(past, jnp.exp(z - cum - later), 0.0)
        acc = acc + _dot(w.astype(BF16), vb)
        later = later + jnp.sum(sp, axis=1, keepdims=True)
        return step + 1, acc, later, jnp.min(later)

    acc0 = jnp.zeros((tile, q.shape[1]), F32)
    later0 = jnp.zeros((tile, 1), F32)
    _, acc, _, _ = lax.while_loop(cond, body, (jnp.int32(0), acc0, later0, jnp.float32(0.0)))
    o_ref[...] = acc.astype(o_ref.dtype)


def stick_breaking_attention(qkvz, bsz, seq, n_heads, *, tile=SB_TILE):
    t = qkvz.shape[0]
    dh = SB_HEAD_DIM
    nq = seq // tile
    kern = functools.partial(_sb_kernel, tile=tile, scale=1.0 / math.sqrt(dh))
    return pl.pallas_call(
        kern,
        grid=(bsz, n_heads, nq),
        in_specs=[pl.BlockSpec((tile, dh), lambda b, h, i: (b * nq + i, h)),
                  pl.BlockSpec((seq, dh), lambda b, h, i: (b, n_heads + h)),
                  pl.BlockSpec((seq, dh), lambda b, h, i: (b, 2 * n_heads + h))],
        out_specs=pl.BlockSpec((tile, dh), lambda b, h, i: (b * nq + i, h)),
        out_shape=jax.ShapeDtypeStruct((t, n_heads * dh), BF16),
        compiler_params=_params("arbitrary", "arbitrary", "arbitrary"),
    )(qkvz, qkvz, qkvz)


def _gelu(x):
    return jax.nn.gelu(x, approximate=True)


def _gm_kernel(z1_ref, z2_ref, lng_ref, ws_ref, bs_ref, o_ref, *, chunk):
    u = _gelu(z1_ref[...].astype(F32))
    v = _gelu(z2_ref[...].astype(F32))
    mu = jnp.mean(v, axis=-1, keepdims=True)
    vc = v - mu
    var = jnp.mean(vc * vc, axis=-1, keepdims=True)
    vn = vc * lax.rsqrt(var + EPS) * lng_ref[0]
    row = lax.broadcasted_iota(jnp.int32, (chunk, chunk), 0)
    col = lax.broadcasted_iota(jnp.int32, (chunk, chunk), 1)
    w = jnp.where(row >= col, ws_ref[0], 0.0)
    bias = bs_ref[0]
    for ci in range(u.shape[0] // chunk):
        sl = slice(ci * chunk, (ci + 1) * chunk)
        mixed = jnp.dot(w, vn[sl], preferred_element_type=F32, precision=lax.Precision.HIGHEST) + bias
        o_ref[sl, :] = (u[sl] * mixed).astype(o_ref.dtype)


def spatial_gating(qkvz, ln_g, w_s, b_s, n_groups, *, rows=GM_ROWS):
    t = qkvz.shape[0]
    c = GM_GROUP_DIM
    kern = functools.partial(_gm_kernel, chunk=GM_CHUNK)
    return pl.pallas_call(
        kern,
        grid=(n_groups, t // rows),
        in_specs=[pl.BlockSpec((rows, c), lambda g, i: (i, 3 * n_groups + g)),
                  pl.BlockSpec((rows, c), lambda g, i: (i, 4 * n_groups + g)),
                  pl.BlockSpec((1, 1, c), lambda g, i: (g, 0, 0)),
                  pl.BlockSpec((1, GM_CHUNK, GM_CHUNK), lambda g, i: (g, 0, 0)),
                  pl.BlockSpec((1, GM_CHUNK, 1), lambda g, i: (g, 0, 0))],
        out_specs=pl.BlockSpec((rows, c), lambda g, i: (i, g)),
        out_shape=jax.ShapeDtypeStruct((t, n_groups * c), BF16),
        compiler_params=_params("arbitrary", "arbitrary"),
    )(qkvz, qkvz, ln_g[:, None, :], w_s, b_s[:, :, None])


def ssm_operators(lam_re, lam_im, log_dt, b_re, b_im, c_re, c_im):
    g_all, p = lam_re.shape
    cg = SSM_GROUP
    L = SSM_CHUNK
    gpt = LANES // cg
    nt = g_all // gpt
    lr = jnp.minimum(lam_re, -1e-4)
    li = lam_im
    dt = jnp.exp(log_dt)[:, None]
    mag = jnp.exp(lr * dt)
    a_re = mag * jnp.cos(li * dt)
    a_im = mag * jnp.sin(li * dt)
    den = lr * lr + li * li
    nr = a_re - 1.0
    f_re = (nr * lr + a_im * li) / den
    f_im = (a_im * lr - nr * li) / den
    bb_re = f_re[:, :, None] * b_re - f_im[:, :, None] * b_im
    bb_im = f_re[:, :, None] * b_im + f_im[:, :, None] * b_re
    tau = jnp.arange(L + 1, dtype=F32)[:, None, None]
    pmag = jnp.exp(lr * dt * tau)
    pw_re = pmag * jnp.cos(li * dt * tau)
    pw_im = pmag * jnp.sin(li * dt * tau)
    ab_re = pw_re[..., None] * bb_re - pw_im[..., None] * bb_im
    ab_im = pw_re[..., None] * bb_im + pw_im[..., None] * bb_re
    kk = (jnp.einsum('gcp,lgpd->lgcd', c_re, ab_re[:L], precision=lax.Precision.HIGHEST)
          - jnp.einsum('gcp,lgpd->lgcd', c_im, ab_im[:L], precision=lax.Precision.HIGHEST))
    eye = jnp.eye(gpt, dtype=F32)
    k_c = jnp.einsum('ljgcd,gh->jlgdhc', kk.reshape(L, nt, gpt, cg, cg), eye).reshape(nt, L, LANES, LANES)
    def state_in(ab):
        v = ab[:L][::-1].reshape(L, nt, gpt, p, cg).transpose(1, 0, 2, 4, 3).reshape(nt, L, LANES, p)
        return jnp.concatenate([v] * (LANES // p), axis=-1)
    b_c = jnp.stack([state_in(ab_re), state_in(ab_im)], axis=2)
    q_re = pw_re[1:]
    q_im = pw_im[1:]
    co_re = c_re[None] * q_re[:, :, None, :] - c_im[None] * q_im[:, :, None, :]
    co_im = -(c_re[None] * q_im[:, :, None, :] + c_im[None] * q_re[:, :, None, :])
    def state_out(co):
        v = co.reshape(L, nt, gpt, cg, p).transpose(1, 0, 4, 2, 3).reshape(nt, L, p, LANES)
        return jnp.concatenate([v] * (LANES // p), axis=-2)
    c_c = jnp.stack([state_out(co_re), state_out(co_im)], axis=2)
    al_re = pw_re[L].reshape(nt, 1, gpt * p)
    al_im = pw_im[L].reshape(nt, 1, gpt * p)
    return k_c.astype(BF16), b_c.astype(BF16), c_c.astype(BF16), al_re, al_im


def _ssm_expand(kc_ref, bc_ref, cc_ref, m_s, b_s, c_s):
    L = kc_ref.shape[1]
    nstate_half = b_s.shape[1] // 2
    pairs = nstate_half // LANES
    row = lax.broadcasted_iota(jnp.int32, (LANES, LANES), 0)
    lane = lax.broadcasted_iota(jnp.int32, (LANES, LANES), 1)
    per_blk = LANES // SSM_STATE
    zero = jnp.zeros((LANES, LANES), BF16)

    @pl.when(pl.program_id(0) == 0)
    def _():
        m_s[...] = jnp.zeros_like(m_s)

    for s_in in range(L):
        for s_out in range(s_in, L):
            m_s[s_in * LANES:(s_in + 1) * LANES, s_out * LANES:(s_out + 1) * LANES] = kc_ref[0, s_out - s_in]
    for pi in range(pairs):
        in_mask = (row // SSM_GROUP) == (per_blk * pi + lane // SSM_STATE)
        out_mask = (lane // SSM_GROUP) == (per_blk * pi + row // SSM_STATE)
        for s in range(L):
            for ri in range(2):
                col = ri * nstate_half + pi * LANES
                b_s[s * LANES:(s + 1) * LANES, col:col + LANES] = jnp.where(in_mask, bc_ref[0, s, ri], zero)
                c_s[col:col + LANES, s * LANES:(s + 1) * LANES] = jnp.where(out_mask, cc_ref[0, s, ri], zero)


def _ssm_kernel(u_ref, kc_ref, bc_ref, cc_ref, are_ref, aim_ref, d_ref, o_ref,
                m_s, b_s, c_s, xl3, xp3, pw3, ybuf, *, bsz, nseg, seglen):
    L = kc_ref.shape[1]
    nrow = u_ref.shape[0] // L
    nblk = xl3.shape[0]
    hb = nblk // 2
    nseq = bsz * nseg
    _ssm_expand(kc_ref, bc_ref, cc_ref, m_s, b_s, c_s)
    u_steps = [u_ref[pl.ds(s, nrow, stride=L), :] for s in range(L)]
    u2 = jnp.concatenate(u_steps, axis=1).astype(BF16)
    xl = _dot(u2, b_s[...])
    for kb in range(nblk):
        xl3[kb] = xl[:, kb * LANES:(kb + 1) * LANES]
    a_re = [are_ref[0, :, kb * LANES:(kb + 1) * LANES] for kb in range(hb)]
    a_im = [aim_ref[0, :, kb * LANES:(kb + 1) * LANES] for kb in range(hb)]

    def step(k, carry):
        xs, ps = carry
        rows = pl.ds(k, nseq, stride=seglen)
        new_x, new_p = [], []
        for kb in range(hb):
            xr, xi = xs[kb]
            pr, pi = ps[kb]
            xp3.at[kb][rows, :] = xr
            xp3.at[hb + kb][rows, :] = xi
            pw3.at[kb][pl.ds(k, 1), :] = pr
            pw3.at[hb + kb][pl.ds(k, 1), :] = pi
            lr = xl3.at[kb][rows, :]
            li = xl3.at[hb + kb][rows, :]
            new_x.append((a_re[kb] * xr - a_im[kb] * xi + lr, a_re[kb] * xi + a_im[kb] * xr + li))
            new_p.append((a_re[kb] * pr - a_im[kb] * pi, a_re[kb] * pi + a_im[kb] * pr))
        return tuple(new_x), tuple(new_p)

    x0 = tuple((jnp.zeros((nseq, LANES), F32), jnp.zeros((nseq, LANES), F32)) for _ in range(hb))
    p0 = tuple((jnp.ones((1, LANES), F32), jnp.zeros((1, LANES), F32)) for _ in range(hb))
    xs, ps = lax.fori_loop(0, seglen, step, (x0, p0))

    for kb in range(hb):
        pr = pw3[kb]
        pi = pw3[hb + kb]
        fr, fi = xs[kb]
        qr, qi = ps[kb]
        for b in range(bsz):
            sr = si = None
            for q in range(1, nseg):
                m = b * nseg + q
                er, ei = fr[m - 1:m], fi[m - 1:m]
                if sr is None:
                    sr, si = er, ei
                else:
                    sr, si = er + qr * sr - qi * si, ei + qr * si + qi * sr
                sl = slice(m * seglen, (m + 1) * seglen)
                xp3[kb, sl, :] = xp3[kb, sl, :] + (pr * sr - pi * si)
                xp3[hb + kb, sl, :] = xp3[hb + kb, sl, :] + (pr * si + pi * sr)

    xp = jnp.concatenate([xp3[kb] for kb in range(nblk)], axis=1).astype(BF16)
    y = _dot(u2, m_s[...]) + _dot(xp, c_s[...])
    d = d_ref[...]
    for s in range(L):
        ys = y[:, s * LANES:(s + 1) * LANES] + d * u_steps[s]
        ybuf[pl.ds(s, nrow, stride=L), :] = _gelu(ys)
    o_ref[...] = ybuf[...].astype(o_ref.dtype)


def ssm_mixer(u, ops, d_skip, bsz):
    k_c, b_c, c_c, al_re, al_im = ops
    t, width = u.shape
    L = k_c.shape[1]
    nrow = t // L
    nt = width // LANES
    nstate = 2 * al_re.shape[2]
    sublanes = 8
    nseg = sublanes // bsz
    seglen = nrow // (bsz * nseg)
    nblk = nstate // LANES
    kern = functools.partial(_ssm_kernel, bsz=bsz, nseg=nseg, seglen=seglen)
    return pl.pallas_call(
        kern,
        grid=(nt,),
        in_specs=[pl.BlockSpec((t, LANES), lambda j: (0, j)),
                  pl.BlockSpec((1, L, LANES, LANES), lambda j: (j, 0, 0, 0)),
                  pl.BlockSpec((1, L, 2, LANES, LANES), lambda j: (j, 0, 0, 0, 0)),
                  pl.BlockSpec((1, L, 2, LANES, LANES), lambda j: (j, 0, 0, 0, 0)),
                  pl.BlockSpec((1, 1, nstate // 2), lambda j: (j, 0, 0)),
                  pl.BlockSpec((1, 1, nstate // 2), lambda j: (j, 0, 0)),
                  pl.BlockSpec((1, LANES), lambda j: (0, j))],
        out_specs=pl.BlockSpec((t, LANES), lambda j: (0, j)),
        out_shape=jax.ShapeDtypeStruct((t, width), BF16),
        scratch_shapes=[pltpu.VMEM((L * LANES, L * LANES), BF16), pltpu.VMEM((L * LANES, nstate), BF16),
                        pltpu.VMEM((nstate, L * LANES), BF16),
                        pltpu.VMEM((nblk, nrow, LANES), F32), pltpu.VMEM((nblk, nrow, LANES), F32),
                        pltpu.VMEM((nblk, seglen, LANES), F32), pltpu.VMEM((t, LANES), F32)],
        compiler_params=_params("arbitrary"),
        name="ssm_mixer",
    )(u, k_c, b_c, c_c, al_re, al_im, d_skip)


def _route_kernel(logit_ref, idx_ref, gate_ref, *, n_experts):
    lg = logit_ref[...]
    lane = lax.broadcasted_iota(jnp.int32, lg.shape, 1)
    neg = jnp.float32(-jnp.inf)
    lg = jnp.where(lane < n_experts, lg, neg)
    m1 = jnp.max(lg, axis=-1, keepdims=True)
    i1 = jnp.min(jnp.where(lg == m1, lane, LANES), axis=-1, keepdims=True)
    lg2 = jnp.where(lane == i1, neg, lg)
    m2 = jnp.max(lg2, axis=-1, keepdims=True)
    i2 = jnp.min(jnp.where(lg2 == m2, lane, LANES), axis=-1, keepdims=True)
    e2 = jnp.exp(m2 - m1)
    p1 = 1.0 / (1.0 + e2)
    p2 = e2 / (1.0 + e2)
    idx_ref[...] = jnp.where(lane == 0, i1, jnp.where(lane == 1, i2, 0))
    gate_ref[...] = jnp.where(lane == 0, p1, jnp.where(lane == 1, p2, 0.0))


def route_top2(logits, *, tm=512):
    t = logits.shape[0]
    kern = functools.partial(_route_kernel, n_experts=N_EXPERTS)
    return pl.pallas_call(
        kern,
        grid=(t // tm,),
        in_specs=[pl.BlockSpec((tm, LANES), lambda i: (i, 0))],
        out_specs=[pl.BlockSpec((tm, LANES), lambda i: (i, 0)), pl.BlockSpec((tm, LANES), lambda i: (i, 0))],
        out_shape=[jax.ShapeDtypeStruct((t, LANES), jnp.int32), jax.ShapeDtypeStruct((t, LANES), F32)],
        compiler_params=_params("arbitrary"),
    )(logits)


def _dispatch_kernel(tok_ref, h_hbm, o_ref, buf, sem, *, tile):
    base = pl.program_id(0) * tile

    def row_copy(r):
        return pltpu.make_async_copy(h_hbm.at[pl.ds(tok_ref[base + r], 1)], buf.at[pl.ds(r, 1)], sem)

    def issue(r, c):
        row_copy(r).start()
        return c

    lax.fori_loop(0, tile, issue, 0)

    def drain(r, c):
        row_copy(r).wait()
        return c

    lax.fori_loop(0, tile, drain, 0)
    o_ref[...] = buf[...].astype(o_ref.dtype)


def moe_dispatch(h, tok_of_slot, n_slots, *, tile=MOE_TILE):
    d = h.shape[1]
    kern = functools.partial(_dispatch_kernel, tile=tile)
    return pl.pallas_call(
        kern,
        grid_spec=pltpu.PrefetchScalarGridSpec(
            num_scalar_prefetch=1,
            grid=(n_slots // tile,),
            in_specs=[pl.BlockSpec(memory_space=pl.ANY)],
            out_specs=pl.BlockSpec((tile, d), lambda i, tok: (i, 0)),
            scratch_shapes=[pltpu.VMEM((tile, d), F32), pltpu.SemaphoreType.DMA(())]),
        out_shape=jax.ShapeDtypeStruct((n_slots, d), BF16),
        compiler_params=_params("arbitrary"),
    )(tok_of_slot, h)


def _expert_cast(tile_expert_ref, refs, scratches):
    i = pl.program_id(1)
    prev = tile_expert_ref[jnp.maximum(i - 1, 0)]
    changed = jnp.logical_or(i == 0, tile_expert_ref[i] != prev)

    @pl.when(changed)
    def _():
        for r, s in zip(refs, scratches):
            s[...] = r[0].astype(BF16)


def _moe_up_kernel(te_ref, na_ref, a_ref, wg_ref, wu_ref, o_ref, wgbf, wubf):
    _expert_cast(te_ref, (wg_ref, wu_ref), (wgbf, wubf))

    @pl.when(pl.program_id(1) < na_ref[0])
    def _():
        a = a_ref[...]
        g = _dot(a, wgbf[...])
        u = _dot(a, wubf[...])
        o_ref[...] = (g * jax.nn.sigmoid(g) * u).astype(o_ref.dtype)

    @pl.when(pl.program_id(1) >= na_ref[0])
    def _():
        o_ref[...] = jnp.zeros_like(o_ref)


def moe_up(hg, tile_expert, n_active, wg, wu, *, tile=MOE_TILE, tn=512):
    r, k = hg.shape
    n = wg.shape[2]
    return pl.pallas_call(
        _moe_up_kernel,
        grid_spec=pltpu.PrefetchScalarGridSpec(
            num_scalar_prefetch=2,
            grid=(n // tn, r // tile),
            in_specs=[pl.BlockSpec((tile, k), lambda j, i, te, na: (i, 0)),
                      pl.BlockSpec((1, k, tn), lambda j, i, te, na: (te[i], 0, j)),
                      pl.BlockSpec((1, k, tn), lambda j, i, te, na: (te[i], 0, j))],
            out_specs=pl.BlockSpec((tile, tn), lambda j, i, te, na: (i, j)),
            scratch_shapes=[pltpu.VMEM((k, tn), BF16), pltpu.VMEM((k, tn), BF16)]),
        out_shape=jax.ShapeDtypeStruct((r, n), BF16),
        compiler_params=_params("arbitrary", "arbitrary"),
    )(tile_expert, n_active, hg, wg, wu)


def _moe_down_kernel(te_ref, na_ref, a_ref, w_ref, o_ref, wbf):
    _expert_cast(te_ref, (w_ref,), (wbf,))

    @pl.when(pl.program_id(1) < na_ref[0])
    def _():
        o_ref[...] = _dot(a_ref[...], wbf[...])

    @pl.when(pl.program_id(1) >= na_ref[0])
    def _():
        o_ref[...] = jnp.zeros_like(o_ref)


def moe_down(act, tile_expert, n_active, wd, *, tile=MOE_TILE, tn=1024):
    r, k = act.shape
    n = wd.shape[2]
    return pl.pallas_call(
        _moe_down_kernel,
        grid_spec=pltpu.PrefetchScalarGridSpec(
            num_scalar_prefetch=2,
            grid=(n // tn, r // tile),
            in_specs=[pl.BlockSpec((tile, k), lambda j, i, te, na: (i, 0)),
                      pl.BlockSpec((1, k, tn), lambda j, i, te, na: (te[i], 0, j))],
            out_specs=pl.BlockSpec((tile, tn), lambda j, i, te, na: (i, j)),
            scratch_shapes=[pltpu.VMEM((k, tn), BF16)]),
        out_shape=jax.ShapeDtypeStruct((r, n), F32),
        compiler_params=_params("arbitrary", "arbitrary"),
    )(tile_expert, n_active, act, wd)


def _combine_kernel(pos_ref, og_hbm, x_ref, gate_ref, mod_ref, fg_ref, o_ref, buf1, buf2, sem, *, tm):
    base = pl.program_id(0) * tm

    def copies(r):
        t = base + r
        return (pltpu.make_async_copy(og_hbm.at[pl.ds(pos_ref[2 * t], 1)], buf1.at[pl.ds(r, 1)], sem),
                pltpu.make_async_copy(og_hbm.at[pl.ds(pos_ref[2 * t + 1], 1)], buf2.at[pl.ds(r, 1)], sem))

    def issue(r, c):
        c1, c2 = copies(r)
        c1.start()
        c2.start()
        return c

    lax.fori_loop(0, tm, issue, 0)

    def drain(r, c):
        c1, c2 = copies(r)
        c1.wait()
        c2.wait()
        return c

    lax.fori_loop(0, tm, drain, 0)
    g = gate_ref[...]
    moe = g[:, 0:1] * buf1[...] + g[:, 1:2] * buf2[...]
    xn = x_ref[...] + mod_ref[0, 2:3, :] * moe
    o_ref[...] = _rms(xn) * fg_ref[...]


def moe_combine_final(og, pos_flat, x, gates, mod, final_g, seq, *, tm=128):
    t, d = x.shape
    per_b = seq // tm
    kern = functools.partial(_combine_kernel, tm=tm)
    return pl.pallas_call(
        kern,
        grid_spec=pltpu.PrefetchScalarGridSpec(
            num_scalar_prefetch=1,
            grid=(t // tm,),
            in_specs=[pl.BlockSpec(memory_space=pl.ANY),
                      pl.BlockSpec((tm, d), lambda i, pos: (i, 0)),
                      pl.BlockSpec((tm, LANES), lambda i, pos: (i, 0)),
                      pl.BlockSpec((1, 3, d), lambda i, pos: (i // per_b, 0, 0)),
                      pl.BlockSpec((1, d), lambda i, pos: (0, 0))],
            out_specs=pl.BlockSpec((tm, d), lambda i, pos: (i, 0)),
            scratch_shapes=[pltpu.VMEM((tm, d), F32), pltpu.VMEM((tm, d), F32), pltpu.SemaphoreType.DMA(())]),
        out_shape=jax.ShapeDtypeStruct((t, d), F32),
        compiler_params=_params("arbitrary"),
    )(pos_flat, og, x, gates, mod, final_g)


def moe_plan(idx, n_tokens, *, tile=MOE_TILE):
    e1 = idx[:, 0]
    e2 = idx[:, 1]
    flat_e = jnp.stack([e1, e2], axis=1).reshape(-1)
    onehot = (flat_e[:, None] == jnp.arange(N_EXPERTS)[None, :]).astype(jnp.int32)
    rank = jnp.cumsum(onehot, axis=0) - onehot
    counts = jnp.sum(onehot, axis=0)
    padded = ((counts + tile - 1) // tile) * tile
    ends = jnp.cumsum(padded)
    starts = ends - padded
    pos = jnp.sum(onehot * (rank + starts[None, :]), axis=1)
    n_slots = 2 * n_tokens + N_EXPERTS * tile
    n_tiles = n_slots // tile
    tok = jnp.repeat(jnp.arange(n_tokens, dtype=jnp.int32), 2)
    tok_of_slot = jnp.zeros((n_slots,), jnp.int32).at[pos].set(tok)
    n_active = (ends[-1] // tile).astype(jnp.int32)
    tile_start = jnp.arange(n_tiles, dtype=jnp.int32) * tile
    tile_expert = jnp.sum((tile_start[:, None] >= ends[None, :]).astype(jnp.int32), axis=1)
    last_expert = tile_expert[jnp.maximum(n_active - 1, 0)]
    tile_expert = jnp.where(jnp.arange(n_tiles) < n_active, tile_expert, last_expert).astype(jnp.int32)
    return pos.astype(jnp.int32), tok_of_slot, tile_expert, n_active.reshape(1), n_slots


def kernel(x, c, mix0_norm_g, mix0_ada_w, mix0_ada_b, mix0_w_in, gm_ln_g, gm_w_s, gm_b_s, mix0_w_out, ffn0_norm_g, ffn0_ada_w, ffn0_ada_b, ffn0_w_gate, ffn0_w_up, ffn0_w_down, mix1_norm_g, mix1_ada_w, mix1_ada_b, ssm_w_in, ssm_lam_re, ssm_lam_im, ssm_log_dt, ssm_b_re, ssm_b_im, ssm_c_re, ssm_c_im, ssm_d, glu_w_a, glu_w_b, moe_norm_g, moe_ada_w, moe_ada_b, moe_w_router, moe_w_gate, moe_w_up, moe_w_down, final_norm_g):
    bsz, seq, d = x.shape
    t = bsz * seq
    n_heads = mix0_w_in.shape[2] // (5 * SB_HEAD_DIM)
    n_groups = gm_w_s.shape[1]
    xf = x.reshape(t, d)
    c_pad = jnp.zeros((8, d), F32).at[:bsz].set(c)
    tm_big = min(1024, seq)

    def ada(w, b):
        m = ada_params(c_pad, w[0], b[0][None, :])
        return m[:bsz].reshape(bsz, 3, d)

    mod = ada(mix0_ada_w, mix0_ada_b)
    h = norm_mod(xf, mix0_norm_g, mod, seq)
    qkvz = matmul(h, mix0_w_in[0], tm=tm_big,tn=512, out_dtype=BF16)
    a_out = stick_breaking_attention(qkvz, bsz, seq, n_heads)
    b_out = spatial_gating(qkvz, gm_ln_g[0], gm_w_s[0], gm_b_s[0], n_groups)
    xf = matmul2_resid(a_out, b_out, mix0_w_out[0], xf, mod, seq, tm=tm_big,tn=512)
    mod = ada(ffn0_ada_w, ffn0_ada_b)
    h = norm_mod(xf, ffn0_norm_g, mod, seq)
    act = matmul_swiglu(h, ffn0_w_gate[0], ffn0_w_up[0], tm=tm_big,tn=256)
    xf = matmul_resid(act, ffn0_w_down[0], xf, mod, seq, tm=256, tn=512)
    mod = ada(mix1_ada_w, mix1_ada_b)
    h = norm_mod(xf, mix1_norm_g, mod, seq)
    u = matmul(h, ssm_w_in[0], tm=tm_big,tn=512, out_dtype=F32)
    ops = ssm_operators(ssm_lam_re[0], ssm_lam_im[0], ssm_log_dt[0], ssm_b_re[0], ssm_b_im[0],
                        ssm_c_re[0], ssm_c_im[0])
    y = ssm_mixer(u, ops, ssm_d, bsz)
    xf = matmul_glu_resid(y, glu_w_a[0], glu_w_b[0], xf, mod, seq, tm=tm_big,tn=256)
    mod = ada(moe_ada_w, moe_ada_b)
    w_router_pad = jnp.zeros((d, LANES), F32).at[:, :N_EXPERTS].set(moe_w_router[0])
    h32, logits = norm_router(xf, moe_norm_g, mod, w_router_pad, seq)
    idx, gates = route_top2(logits)
    pos, tok_of_slot, tile_expert, n_active, n_slots = moe_plan(idx, t)
    hg = moe_dispatch(h32, tok_of_slot, n_slots)
    act = moe_up(hg, tile_expert, n_active, moe_w_gate[0], moe_w_up[0])
    og = moe_down(act, tile_expert, n_active, moe_w_down[0])
    out = moe_combine_final(og, pos, xf, gates, mod, final_norm_g[None, :], seq)
    return out.reshape(bsz, seq, d)
```

```python
import functools
import math

import jax
import jax.numpy as jnp
from jax import lax
from jax.experimental import pallas as pl
from jax.experimental.pallas import tpu as pltpu

EPS = 1e-6
SB_HEAD_DIM = 128
SB_TILE = 256
GM_GROUP_DIM = 128
GM_CHUNK = 128
GM_ROWS = 2048
SSM_GROUP = 16
SSM_STATE = 64
SSM_CHUNK = 8
LANES = 128
N_EXPERTS = 8
MOE_TILE = 256
MOE_DOWN_TN = 1024
SB_SKIP_EXPONENT = 104.0
VMEM_LIMIT_BYTES = 56 * 1024 * 1024

BF16 = jnp.bfloat16
F32 = jnp.float32


def _params(*sem):
    return pltpu.CompilerParams(dimension_semantics=sem, vmem_limit_bytes=VMEM_LIMIT_BYTES)


def _dot(a, b):
    return jnp.dot(a, b, preferred_element_type=F32)


def _ada_kernel(c_ref, w_ref, b_ref, o_ref):
    c = c_ref[...]
    s = (c * jax.nn.sigmoid(c))
    s_hi = s.astype(BF16)
    s_lo = (s - s_hi.astype(F32)).astype(BF16)
    w = w_ref[...].astype(BF16)
    o_ref[...] = _dot(s_hi, w) + _dot(s_lo, w) + b_ref[...]


def ada_params(c_pad, w, b, *, tn=512):
    rows, d = c_pad.shape
    n = w.shape[1]
    return pl.pallas_call(
        _ada_kernel,
        grid=(n // tn,),
        in_specs=[pl.BlockSpec((rows, d), lambda j: (0, 0)),
                  pl.BlockSpec((d, tn), lambda j: (0, j)),
                  pl.BlockSpec((1, tn), lambda j: (0, j))],
        out_specs=pl.BlockSpec((rows, tn), lambda j: (0, j)),
        out_shape=jax.ShapeDtypeStruct((rows, n), F32),
        compiler_params=_params("arbitrary"),
        name="ada_params",
    )(c_pad, w, b)


def _rms(x):
    return x * lax.rsqrt(jnp.mean(x * x, axis=-1, keepdims=True) + EPS)


def _norm_mod_kernel(x_ref, g_ref, mod_ref, o_ref):
    y = _rms(x_ref[...]) * g_ref[...]
    o_ref[...] = (y * (1.0 + mod_ref[0, 1:2, :]) + mod_ref[0, 0:1, :]).astype(o_ref.dtype)


def norm_mod(x, g, mod, seq, *, tm=256, out_dtype=BF16):
    t, d = x.shape
    per_b = seq // tm
    return pl.pallas_call(
        _norm_mod_kernel,
        grid=(t // tm,),
        in_specs=[pl.BlockSpec((tm, d), lambda i: (i, 0)),
                  pl.BlockSpec((1, d), lambda i: (0, 0)),
                  pl.BlockSpec((1, 3, d), lambda i: (i // per_b, 0, 0))],
        out_specs=pl.BlockSpec((tm, d), lambda i: (i, 0)),
        out_shape=jax.ShapeDtypeStruct((t, d), out_dtype),
        compiler_params=_params("arbitrary"),
        name="norm_mod",
    )(x, g, mod)


def _pack_bf16_pair(hi, lo):
    hb = lax.bitcast_convert_type(hi.astype(BF16).astype(F32), jnp.uint32)
    lb = lax.bitcast_convert_type(lo.astype(BF16).astype(F32), jnp.uint32)
    return hb | lax.shift_right_logical(lb, jnp.uint32(16))


def _unpack_bf16_pair(u):
    hi = lax.bitcast_convert_type(u & jnp.uint32(0xFFFF0000), F32)
    lo = lax.bitcast_convert_type(lax.shift_left(u, jnp.uint32(16)), F32)
    return hi, lo


def _norm_router_kernel(x_ref, g_ref, mod_ref, wr_ref, h_ref, logit_ref):
    y = _rms(x_ref[...]) * g_ref[...]
    h = y * (1.0 + mod_ref[0, 1:2, :]) + mod_ref[0, 0:1, :]
    half = h.shape[1] // 2
    h_ref[...] = _pack_bf16_pair(h[:, :half], h[:, half:])
    logit_ref[...] = jnp.dot(h, wr_ref[...], preferred_element_type=F32, precision=lax.Precision.HIGHEST)


def norm_router(x, g, mod, w_router_pad, seq, *, tm=256):
    t, d = x.shape
    per_b = seq // tm
    return pl.pallas_call(
        _norm_router_kernel,
        grid=(t // tm,),
        in_specs=[pl.BlockSpec((tm, d), lambda i: (i, 0)),
                  pl.BlockSpec((1, d), lambda i: (0, 0)),
                  pl.BlockSpec((1, 3, d), lambda i: (i // per_b, 0, 0)),
                  pl.BlockSpec((d, LANES), lambda i: (0, 0))],
        out_specs=[pl.BlockSpec((tm, d // 2), lambda i: (i, 0)),
                   pl.BlockSpec((tm, LANES), lambda i: (i, 0))],
        out_shape=[jax.ShapeDtypeStruct((t, d // 2), jnp.uint32), jax.ShapeDtypeStruct((t, LANES), F32)],
        compiler_params=_params("arbitrary"),
        name="norm_router",
    )(x, g, mod, w_router_pad)


def _final_norm_kernel(x_ref, g_ref, o_ref):
    o_ref[...] = _rms(x_ref[...]) * g_ref[...]


def final_norm(x, g, *, tm=256):
    t, d = x.shape
    return pl.pallas_call(
        _final_norm_kernel,
        grid=(t // tm,),
        in_specs=[pl.BlockSpec((tm, d), lambda i: (i, 0)), pl.BlockSpec((1, d), lambda i: (0, 0))],
        out_specs=pl.BlockSpec((tm, d), lambda i: (i, 0)),
        out_shape=jax.ShapeDtypeStruct((t, d), F32),
        compiler_params=_params("arbitrary"),
    )(x, g)


def _cast_once(w_ref, wbf_ref):
    @pl.when(pl.program_id(1) == 0)
    def _():
        wbf_ref[...] = w_ref[...].astype(BF16)


def _mm_kernel(a_ref, w_ref, o_ref, wbf):
    _cast_once(w_ref, wbf)
    o_ref[...] = _dot(a_ref[...], wbf[...]).astype(o_ref.dtype)


def matmul(a, w, *, tm, tn, out_dtype):
    m, k = a.shape
    n = w.shape[1]
    return pl.pallas_call(
        _mm_kernel,
        grid=(n // tn, m // tm),
        in_specs=[pl.BlockSpec((tm, k), lambda j, i: (i, 0)),
                  pl.BlockSpec((k, tn), lambda j, i: (0, j))],
        out_specs=pl.BlockSpec((tm, tn), lambda j, i: (i, j)),
        out_shape=jax.ShapeDtypeStruct((m, n), out_dtype),
        scratch_shapes=[pltpu.VMEM((k, tn), BF16)],
        compiler_params=_params("arbitrary", "arbitrary"),
        name="mm",
    )(a, w)


def _mm_resid_kernel(a_ref, w_ref, x_ref, mod_ref, o_ref, wbf):
    _cast_once(w_ref, wbf)
    o_ref[...] = x_ref[...] + mod_ref[0, 2:3, :] * _dot(a_ref[...], wbf[...])


def matmul_resid(a, w, x, mod, seq, *, tm, tn):
    m, k = a.shape
    n = w.shape[1]
    per_b = seq // tm
    return pl.pallas_call(
        _mm_resid_kernel,
        grid=(n // tn, m // tm),
        in_specs=[pl.BlockSpec((tm, k), lambda j, i: (i, 0)),
                  pl.BlockSpec((k, tn), lambda j, i: (0, j), pipeline_mode=pl.Buffered(1)),
                  pl.BlockSpec((tm, tn), lambda j, i: (i, j)),
                  pl.BlockSpec((1, 3, tn), lambda j, i: (i // per_b, 0, j))],
        out_specs=pl.BlockSpec((tm, tn), lambda j, i: (i, j)),
        out_shape=jax.ShapeDtypeStruct((m, n), F32),
        scratch_shapes=[pltpu.VMEM((k, tn), BF16)],
        compiler_params=_params("arbitrary", "arbitrary"),
        name="mm_resid",
    )(a, w, x, mod)


def _mm2_resid_kernel(a_ref, b_ref, w_ref, x_ref, mod_ref, o_ref, wbf):
    _cast_once(w_ref, wbf)
    ka = a_ref.shape[1]
    acc = _dot(a_ref[...], wbf[:ka, :]) + _dot(b_ref[...], wbf[ka:, :])
    o_ref[...] = x_ref[...] + mod_ref[0, 2:3, :] * acc


def matmul2_resid(a, b, w, x, mod, seq, *, tm, tn):
    m, ka = a.shape
    kb = b.shape[1]
    n = w.shape[1]
    per_b = seq // tm
    return pl.pallas_call(
        _mm2_resid_kernel,
        grid=(n // tn, m // tm),
        in_specs=[pl.BlockSpec((tm, ka), lambda j, i: (i, 0)),
                  pl.BlockSpec((tm, kb), lambda j, i: (i, 0)),
                  pl.BlockSpec((ka + kb, tn), lambda j, i: (0, j)),
                  pl.BlockSpec((tm, tn), lambda j, i: (i, j)),
                  pl.BlockSpec((1, 3, tn), lambda j, i: (i // per_b, 0, j))],
        out_specs=pl.BlockSpec((tm, tn), lambda j, i: (i, j)),
        out_shape=jax.ShapeDtypeStruct((m, n), F32),
        scratch_shapes=[pltpu.VMEM((ka + kb, tn), BF16)],
        compiler_params=_params("arbitrary", "arbitrary"),
        name="mm2_resid",
    )(a, b, w, x, mod)


def _cast2_once(w1_ref, w2_ref, w1bf, w2bf):
    @pl.when(pl.program_id(1) == 0)
    def _():
        w1bf[...] = w1_ref[...].astype(BF16)
        w2bf[...] = w2_ref[...].astype(BF16)


def _mm_swiglu_kernel(a_ref, wg_ref, wu_ref, o_ref, wgbf, wubf):
    _cast2_once(wg_ref, wu_ref, wgbf, wubf)
    a = a_ref[...]
    g = _dot(a, wgbf[...])
    u = _dot(a, wubf[...])
    o_ref[...] = (g * jax.nn.sigmoid(g) * u).astype(o_ref.dtype)


def matmul_swiglu(a, wg, wu, *, tm, tn):
    m, k = a.shape
    n = wg.shape[1]
    return pl.pallas_call(
        _mm_swiglu_kernel,
        grid=(n // tn, m // tm),
        in_specs=[pl.BlockSpec((tm, k), lambda j, i: (i, 0)),
                  pl.BlockSpec((k, tn), lambda j, i: (0, j)),
                  pl.BlockSpec((k, tn), lambda j, i: (0, j))],
        out_specs=pl.BlockSpec((tm, tn), lambda j, i: (i, j)),
        out_shape=jax.ShapeDtypeStruct((m, n), BF16),
        scratch_shapes=[pltpu.VMEM((k, tn), BF16), pltpu.VMEM((k, tn), BF16)],
        compiler_params=_params("arbitrary", "arbitrary"),
        name="mm_swiglu",
    )(a, wg, wu)


def _mm_glu_resid_kernel(a_ref, wa_ref, wb_ref, x_ref, mod_ref, o_ref, wabf, wbbf):
    _cast2_once(wa_ref, wb_ref, wabf, wbbf)
    a = a_ref[...]
    p = _dot(a, wabf[...])
    q = _dot(a, wbbf[...])
    o_ref[...] = x_ref[...] + mod_ref[0, 2:3, :] * (p * jax.nn.sigmoid(q))


def matmul_glu_resid(a, wa, wb, x, mod, seq, *, tm, tn):
    m, k = a.shape
    n = wa.shape[1]
    per_b = seq // tm
    return pl.pallas_call(
        _mm_glu_resid_kernel,
        grid=(n // tn, m // tm),
        in_specs=[pl.BlockSpec((tm, k), lambda j, i: (i, 0)),
                  pl.BlockSpec((k, tn), lambda j, i: (0, j)),
                  pl.BlockSpec((k, tn), lambda j, i: (0, j)),
                  pl.BlockSpec((tm, tn), lambda j, i: (i, j)),
                  pl.BlockSpec((1, 3, tn), lambda j, i: (i // per_b, 0, j))],
        out_specs=pl.BlockSpec((tm, tn), lambda j, i: (i, j)),
        out_shape=jax.ShapeDtypeStruct((m, n), F32),
        scratch_shapes=[pltpu.VMEM((k, tn), BF16), pltpu.VMEM((k, tn), BF16)],
        compiler_params=_params("arbitrary", "arbitrary"),
        name="mm_glu",
    )(a, wa, wb, x, mod)


def _sb_kernel(q_ref, k_ref, v_ref, o_ref, *, tile, scale, heads):
    qi = pl.program_id(2)
    dh = q_ref.shape[1] // heads
    row = lax.broadcasted_iota(jnp.int32, (tile, tile), 0)
    col = lax.broadcasted_iota(jnp.int32, (tile, tile), 1)
    revcum = jnp.where(row >= col, 1.0, 0.0).astype(BF16)

    def cond(carry):
        step, _, min_later = carry
        return jnp.logical_and(step <= qi, min_later <= SB_SKIP_EXPONENT)

    def body(carry):
        step, state, _ = carry
        kj = qi - step
        start = pl.multiple_of(kj * tile, tile)
        past = (col + kj * tile) < (row + qi * tile)
        new_state = []
        min_later = None
        for h in range(heads):
            acc, later = state[h]
            lanes = slice(h * dh, (h + 1) * dh)
            kb = k_ref[pl.ds(start, tile), lanes]
            vb = v_ref[pl.ds(start, tile), lanes]
            z = lax.dot_general(q_ref[:, lanes], kb, (((1,), (1,)), ((), ())), preferred_element_type=F32) * scale
            sp = jnp.where(past, jnp.maximum(z, 0.0) + jnp.log1p(jnp.exp(-jnp.abs(z))), 0.0)
            sp_hi = sp.astype(BF16)
            sp_lo = (sp - sp_hi.astype(F32)).astype(BF16)
            cum = _dot(sp_hi, revcum) + _dot(sp_lo, revcum)
            w = jnp.where(past, jnp.exp(z - cum - later), 0.0)
            acc = acc + _dot(w.astype(BF16), vb)
            later = later + jnp.sum(sp, axis=1, keepdims=True)
            new_state.append((acc, later))
            m = jnp.min(later)
            min_later = m if min_later is None else jnp.minimum(min_later, m)
        return step + 1, tuple(new_state), min_later

    state0 = tuple((jnp.zeros((tile, dh), F32), jnp.zeros((tile, 1), F32)) for _ in range(heads))
    _, state, _ = lax.while_loop(cond, body, (jnp.int32(0), state0, jnp.float32(0.0)))
    for h in range(heads):
        o_ref[:, h * dh:(h + 1) * dh] = state[h][0].astype(o_ref.dtype)


def stick_breaking_attention(qkvz, bsz, seq, n_heads, *, tile=SB_TILE, heads=2):
    t = qkvz.shape[0]
    dh = SB_HEAD_DIM
    nq = seq // tile
    hg = n_heads // heads
    kern = functools.partial(_sb_kernel, tile=tile, scale=1.0 / math.sqrt(dh), heads=heads)
    return pl.pallas_call(
        kern,
        grid=(bsz, hg, nq),
        in_specs=[pl.BlockSpec((tile, heads * dh), lambda b, h, i: (b * nq + i, h)),
                  pl.BlockSpec((seq, heads * dh), lambda b, h, i: (b, hg + h)),
                  pl.BlockSpec((seq, heads * dh), lambda b, h, i: (b, 2 * hg + h))],
        out_specs=pl.BlockSpec((tile, heads * dh), lambda b, h, i: (b * nq + i, h)),
        out_shape=jax.ShapeDtypeStruct((t, n_heads * dh), BF16),
        compiler_params=_params("arbitrary", "arbitrary", "arbitrary"),
        name="sb_attention",
    )(qkvz, qkvz, qkvz)


def _gelu(x):
    return jax.nn.gelu(x, approximate=True)


def _gm_kernel(z1_ref, z2_ref, lng_ref, ws_ref, bs_ref, o_ref, *, chunk):
    u = _gelu(z1_ref[...].astype(F32))
    v = _gelu(z2_ref[...].astype(F32))
    mu = jnp.mean(v, axis=-1, keepdims=True)
    vc = v - mu
    var = jnp.mean(vc * vc, axis=-1, keepdims=True)
    vn = vc * lax.rsqrt(var + EPS) * lng_ref[0]
    row = lax.broadcasted_iota(jnp.int32, (chunk, chunk), 0)
    col = lax.broadcasted_iota(jnp.int32, (chunk, chunk), 1)
    w = jnp.where(row >= col, ws_ref[0], 0.0)
    bias = bs_ref[0]
    for ci in range(u.shape[0] // chunk):
        sl = slice(ci * chunk, (ci + 1) * chunk)
        mixed = jnp.dot(w, vn[sl], preferred_element_type=F32, precision=lax.Precision.HIGHEST) + bias
        o_ref[sl, :] = (u[sl] * mixed).astype(o_ref.dtype)


def spatial_gating(qkvz, ln_g, w_s, b_s, n_groups, *, rows=GM_ROWS):
    t = qkvz.shape[0]
    c = GM_GROUP_DIM
    rows = min(rows, t)
    kern = functools.partial(_gm_kernel, chunk=GM_CHUNK)
    return pl.pallas_call(
        kern,
        grid=(n_groups, t // rows),
        in_specs=[pl.BlockSpec((rows, c), lambda g, i: (i, 3 * n_groups + g)),
                  pl.BlockSpec((rows, c), lambda g, i: (i, 4 * n_groups + g)),
                  pl.BlockSpec((1, 1, c), lambda g, i: (g, 0, 0)),
                  pl.BlockSpec((1, GM_CHUNK, GM_CHUNK), lambda g, i: (g, 0, 0)),
                  pl.BlockSpec((1, GM_CHUNK, 1), lambda g, i: (g, 0, 0))],
        out_specs=pl.BlockSpec((rows, c), lambda g, i: (i, g)),
        out_shape=jax.ShapeDtypeStruct((t, n_groups * c), BF16),
        compiler_params=_params("arbitrary", "arbitrary"),
        name="spatial_gating",
    )(qkvz, qkvz, ln_g[:, None, :], w_s, b_s[:, :, None])


def ssm_operators(lam_re, lam_im, log_dt, b_re, b_im, c_re, c_im):
    g_all, p = lam_re.shape
    cg = SSM_GROUP
    L = SSM_CHUNK
    gpt = LANES // cg
    nt = g_all // gpt
    lr = jnp.minimum(lam_re, -1e-4)
    li = lam_im
    dt = jnp.exp(log_dt)[:, None]
    mag = jnp.exp(lr * dt)
    a_re = mag * jnp.cos(li * dt)
    a_im = mag * jnp.sin(li * dt)
    den = lr * lr + li * li
    nr = a_re - 1.0
    f_re = (nr * lr + a_im * li) / den
    f_im = (a_im * lr - nr * li) / den
    bb_re = f_re[:, :, None] * b_re - f_im[:, :, None] * b_im
    bb_im = f_re[:, :, None] * b_im + f_im[:, :, None] * b_re
    tau = jnp.arange(L + 1, dtype=F32)[:, None, None]
    pmag = jnp.exp(lr * dt * tau)
    pw_re = pmag * jnp.cos(li * dt * tau)
    pw_im = pmag * jnp.sin(li * dt * tau)
    ab_re = pw_re[..., None] * bb_re - pw_im[..., None] * bb_im
    ab_im = pw_re[..., None] * bb_im + pw_im[..., None] * bb_re
    kk = (jnp.einsum('gcp,lgpd->lgcd', c_re, ab_re[:L], precision=lax.Precision.HIGHEST)
          - jnp.einsum('gcp,lgpd->lgcd', c_im, ab_im[:L], precision=lax.Precision.HIGHEST))
    eye = jnp.eye(gpt, dtype=F32)
    k_c = jnp.einsum('ljgcd,gh->jlgdhc', kk.reshape(L, nt, gpt, cg, cg), eye).reshape(nt, L, LANES, LANES)
    def state_in(ab):
        v = ab[:L][::-1].reshape(L, nt, gpt, p, cg).transpose(1, 0, 2, 4, 3).reshape(nt, L, LANES, p)
        return jnp.concatenate([v] * (LANES // p), axis=-1)
    b_c = jnp.stack([state_in(ab_re), state_in(ab_im)], axis=2)
    q_re = pw_re[1:]
    q_im = pw_im[1:]
    co_re = c_re[None] * q_re[:, :, None, :] - c_im[None] * q_im[:, :, None, :]
    co_im = -(c_re[None] * q_im[:, :, None, :] + c_im[None] * q_re[:, :, None, :])
    def state_out(co):
        v = co.reshape(L, nt, gpt, cg, p).transpose(1, 0, 4, 2, 3).reshape(nt, L, p, LANES)
        return jnp.concatenate([v] * (LANES // p), axis=-2)
    c_c = jnp.stack([state_out(co_re), state_out(co_im)], axis=2)
    al_re = pw_re[L].reshape(nt, 1, gpt * p)
    al_im = pw_im[L].reshape(nt, 1, gpt * p)
    return k_c.astype(BF16), b_c.astype(BF16), c_c.astype(BF16), al_re, al_im


def _ssm_expand(kc_ref, bc_ref, cc_ref, m_s, b_s, c_s):
    L = kc_ref.shape[1]
    nstate_half = b_s.shape[1] // 2
    pairs = nstate_half // LANES
    row = lax.broadcasted_iota(jnp.int32, (LANES, LANES), 0)
    lane = lax.broadcasted_iota(jnp.int32, (LANES, LANES), 1)
    per_blk = LANES // SSM_STATE
    zero = jnp.zeros((LANES, LANES), BF16)

    @pl.when(pl.program_id(0) == 0)
    def _():
        m_s[...] = jnp.zeros_like(m_s)

    for s_in in range(L):
        for s_out in range(s_in, L):
            m_s[s_in * LANES:(s_in + 1) * LANES, s_out * LANES:(s_out + 1) * LANES] = kc_ref[0, s_out - s_in]
    for pi in range(pairs):
        in_mask = (row // SSM_GROUP) == (per_blk * pi + lane // SSM_STATE)
        out_mask = (lane // SSM_GROUP) == (per_blk * pi + row // SSM_STATE)
        for s in range(L):
            for ri in range(2):
                col = ri * nstate_half + pi * LANES
                b_s[s * LANES:(s + 1) * LANES, col:col + LANES] = jnp.where(in_mask, bc_ref[0, s, ri], zero)
                c_s[col:col + LANES, s * LANES:(s + 1) * LANES] = jnp.where(out_mask, cc_ref[0, s, ri], zero)


def _ssm_kernel(u_ref, kc_ref, bc_ref, cc_ref, are_ref, aim_ref, d_ref, o_ref,
                m_s, b_s, c_s, xl3, xp3, pw3, ybuf, *, bsz, nseg, seglen, pitch):
    L = kc_ref.shape[1]
    nblk = xl3.shape[0]
    hb = nblk // 2
    nseq = bsz * nseg
    _ssm_expand(kc_ref, bc_ref, cc_ref, m_s, b_s, c_s)

    @pl.when(pl.program_id(0) == 0)
    def _():
        xp3[...] = jnp.zeros_like(xp3)

    zpad = jnp.zeros((pitch - seglen, LANES), F32)
    u_steps = []
    for s in range(L):
        pieces = []
        for m in range(nseq):
            pieces += [u_ref[pl.ds(m * seglen * L + s, seglen, stride=L), :], zpad]
        u_steps.append(jnp.concatenate(pieces, axis=0))
    u2 = jnp.concatenate(u_steps, axis=1).astype(BF16)
    xl = _dot(u2, b_s[...])
    for kb in range(nblk):
        xl3[kb] = xl[:, kb * LANES:(kb + 1) * LANES]
    a_re = [are_ref[0, :, kb * LANES:(kb + 1) * LANES] for kb in range(hb)]
    a_im = [aim_ref[0, :, kb * LANES:(kb + 1) * LANES] for kb in range(hb)]

    seg_pow = []
    for kb in range(hb):
        pw3[kb, 0:1, :] = jnp.ones((1, LANES), F32)
        pw3[hb + kb, 0:1, :] = jnp.zeros((1, LANES), F32)
        qr, qi = a_re[kb], a_im[kb]
        n = 1
        while n < seglen:
            pr, pi = pw3[kb, 0:n, :], pw3[hb + kb, 0:n, :]
            pw3[kb, n:2 * n, :] = pr * qr - pi * qi
            pw3[hb + kb, n:2 * n, :] = pr * qi + pi * qr
            qr, qi = qr * qr - qi * qi, 2.0 * qr * qi
            n *= 2
        seg_pow.append((qr, qi))

    def step(k, xs):
        rows = pl.ds(k, nseq, stride=pitch)
        new_x = []
        for kb in range(hb):
            xr, xi = xs[kb]
            xp3.at[kb][rows, :] = xr
            xp3.at[hb + kb][rows, :] = xi
            lr = xl3.at[kb][rows, :]
            li = xl3.at[hb + kb][rows, :]
            new_x.append((a_re[kb] * xr - a_im[kb] * xi + lr, a_re[kb] * xi + a_im[kb] * xr + li))
        return tuple(new_x)

    x0 = tuple((jnp.zeros((nseq, LANES), F32), jnp.zeros((nseq, LANES), F32)) for _ in range(hb))
    xs = lax.fori_loop(0, seglen, step, x0)

    for kb in range(hb):
        pr = pw3[kb]
        pi = pw3[hb + kb]
        fr, fi = xs[kb]
        qr, qi = seg_pow[kb]
        for b in range(bsz):
            sr = si = None
            for q in range(1, nseg):
                m = b * nseg + q
                er, ei = fr[m - 1:m], fi[m - 1:m]
                if sr is None:
                    sr, si = er, ei
                else:
                    sr, si = er + qr * sr - qi * si, ei + qr * si + qi * sr
                sl = slice(m * pitch, m * pitch + seglen)
                xp3[kb, sl, :] = xp3[kb, sl, :] + (pr * sr - pi * si)
                xp3[hb + kb, sl, :] = xp3[hb + kb, sl, :] + (pr * si + pi * sr)

    xp = jnp.concatenate([xp3[kb] for kb in range(nblk)], axis=1).astype(BF16)
    y = _dot(u2, m_s[...]) + _dot(xp, c_s[...])
    d = d_ref[...]
    for s in range(L):
        ys = _gelu(y[:, s * LANES:(s + 1) * LANES] + d * u_steps[s])
        for m in range(nseq):
            ybuf[pl.ds(m * seglen * L + s, seglen, stride=L), :] = ys[m * pitch:m * pitch + seglen]
    o_ref[...] = ybuf[...].astype(o_ref.dtype)


def ssm_mixer(u, ops, d_skip, bsz):
    k_c, b_c, c_c, al_re, al_im = ops
    t, width = u.shape
    L = k_c.shape[1]
    nrow = t // L
    nt = width // LANES
    nstate = 2 * al_re.shape[2]
    sublanes = 8
    nseg = sublanes // bsz
    seglen = nrow // (bsz * nseg)
    nblk = nstate // LANES
    assert seglen & (seglen - 1) == 0 and nrow == bsz * nseg * seglen
    pitch = seglen + sublanes
    prow = bsz * nseg * pitch
    kern = functools.partial(_ssm_kernel, bsz=bsz, nseg=nseg, seglen=seglen, pitch=pitch)
    return pl.pallas_call(
        kern,
        grid=(nt,),
        in_specs=[pl.BlockSpec((t, LANES), lambda j: (0, j)),
                  pl.BlockSpec((1, L, LANES, LANES), lambda j: (j, 0, 0, 0)),
                  pl.BlockSpec((1, L, 2, LANES, LANES), lambda j: (j, 0, 0, 0, 0)),
                  pl.BlockSpec((1, L, 2, LANES, LANES), lambda j: (j, 0, 0, 0, 0)),
                  pl.BlockSpec((1, 1, nstate // 2), lambda j: (j, 0, 0)),
                  pl.BlockSpec((1, 1, nstate // 2), lambda j: (j, 0, 0)),
                  pl.BlockSpec((1, LANES), lambda j: (0, j))],
        out_specs=pl.BlockSpec((t, LANES), lambda j: (0, j)),
        out_shape=jax.ShapeDtypeStruct((t, width), BF16),
        scratch_shapes=[pltpu.VMEM((L * LANES, L * LANES), BF16), pltpu.VMEM((L * LANES, nstate), BF16),
                        pltpu.VMEM((nstate, L * LANES), BF16),
                        pltpu.VMEM((nblk, prow, LANES), F32), pltpu.VMEM((nblk, prow, LANES), F32),
                        pltpu.VMEM((nblk, seglen, LANES), F32), pltpu.VMEM((t, LANES), F32)],
        compiler_params=_params("arbitrary"),
        name="ssm_mixer",
    )(u, k_c, b_c, c_c, al_re, al_im, d_skip)


def _route_kernel(logit_ref, idx_ref, gate_ref, *, n_experts):
    lg = logit_ref[...]
    lane = lax.broadcasted_iota(jnp.int32, lg.shape, 1)
    neg = jnp.float32(-jnp.inf)
    lg = jnp.where(lane < n_experts, lg, neg)
    m1 = jnp.max(lg, axis=-1, keepdims=True)
    i1 = jnp.min(jnp.where(lg == m1, lane, LANES), axis=-1, keepdims=True)
    lg2 = jnp.where(lane == i1, neg, lg)
    m2 = jnp.max(lg2, axis=-1, keepdims=True)
    i2 = jnp.min(jnp.where(lg2 == m2, lane, LANES), axis=-1, keepdims=True)
    e2 = jnp.exp(m2 - m1)
    p1 = 1.0 / (1.0 + e2)
    p2 = e2 / (1.0 + e2)
    idx_ref[...] = jnp.where(lane == 0, i1, jnp.where(lane == 1, i2, 0))
    gate_ref[...] = jnp.where(lane == 0, p1, jnp.where(lane == 1, p2, 0.0))


def route_top2(logits, *, tm=512):
    t = logits.shape[0]
    kern = functools.partial(_route_kernel, n_experts=N_EXPERTS)
    return pl.pallas_call(
        kern,
        grid=(t // tm,),
        in_specs=[pl.BlockSpec((tm, LANES), lambda i: (i, 0))],
        out_specs=[pl.BlockSpec((tm, LANES), lambda i: (i, 0)), pl.BlockSpec((tm, LANES), lambda i: (i, 0))],
        out_shape=[jax.ShapeDtypeStruct((t, LANES), jnp.int32), jax.ShapeDtypeStruct((t, LANES), F32)],
        compiler_params=_params("arbitrary"),
    )(logits)


def _dispatch_kernel(tok_ref, h_hbm, o_hbm, sems, *, tile):
    i = pl.program_id(0)

    def row_copy(step, r):
        slot = step * tile + r
        return pltpu.make_async_copy(h_hbm.at[pl.ds(tok_ref[slot], 1)], o_hbm.at[pl.ds(slot, 1)],
                                     sems.at[step % 2])

    def issue(r, c):
        row_copy(i, r).start()
        return c

    lax.fori_loop(0, tile, issue, 0)

    def drain_step(step):
        def drain(r, c):
            row_copy(step, r).wait()
            return c
        lax.fori_loop(0, tile, drain, 0)

    @pl.when(i > 0)
    def _():
        drain_step(i - 1)

    @pl.when(i == pl.num_programs(0) - 1)
    def _():
        drain_step(i)


def moe_dispatch(hp, tok_of_slot, n_slots, *, tile=MOE_TILE):
    d = hp.shape[1]
    kern = functools.partial(_dispatch_kernel, tile=tile)
    return pl.pallas_call(
        kern,
        grid_spec=pltpu.PrefetchScalarGridSpec(
            num_scalar_prefetch=1,
            grid=(n_slots // tile,),
            in_specs=[pl.BlockSpec(memory_space=pl.ANY)],
            out_specs=pl.BlockSpec(memory_space=pl.ANY),
            scratch_shapes=[pltpu.SemaphoreType.DMA((2,))]),
        out_shape=jax.ShapeDtypeStruct((n_slots, d), hp.dtype),
        compiler_params=_params("arbitrary"),
        name="moe_dispatch",
    )(tok_of_slot, hp)


def _expert_cast(tile_expert_ref, refs, scratches):
    i = pl.program_id(1)
    prev = tile_expert_ref[jnp.maximum(i - 1, 0)]
    changed = jnp.logical_or(i == 0, tile_expert_ref[i] != prev)

    @pl.when(changed)
    def _():
        for r, s in zip(refs, scratches):
            s[...] = r[0].astype(BF16)


def _moe_up_kernel(te_ref, na_ref, a_ref, wg_ref, wu_ref, o_ref, wgbf, wubf):
    _expert_cast(te_ref, (wg_ref, wu_ref), (wgbf, wubf))

    @pl.when(pl.program_id(1) < na_ref[0])
    def _():
        hi, lo = _unpack_bf16_pair(a_ref[...])
        a = jnp.concatenate([hi.astype(BF16), lo.astype(BF16)], axis=1)
        g = _dot(a, wgbf[...])
        u = _dot(a, wubf[...])
        o_ref[...] = (g * jax.nn.sigmoid(g) * u).astype(o_ref.dtype)

    @pl.when(pl.program_id(1) >= na_ref[0])
    def _():
        o_ref[...] = jnp.zeros_like(o_ref)


def moe_up(hg, tile_expert, n_active, wg, wu, *, tile=MOE_TILE, tn=512):
    r = hg.shape[0]
    k = wg.shape[1]
    n = wg.shape[2]
    return pl.pallas_call(
        _moe_up_kernel,
        grid_spec=pltpu.PrefetchScalarGridSpec(
            num_scalar_prefetch=2,
            grid=(n // tn, r // tile),
            in_specs=[pl.BlockSpec((tile, k // 2), lambda j, i, te, na: (i, 0)),
                      pl.BlockSpec((1, k, tn), lambda j, i, te, na: (te[i], 0, j)),
                      pl.BlockSpec((1, k, tn), lambda j, i, te, na: (te[i], 0, j))],
            out_specs=pl.BlockSpec((tile, tn), lambda j, i, te, na: (i, j)),
            scratch_shapes=[pltpu.VMEM((k, tn), BF16), pltpu.VMEM((k, tn), BF16)]),
        out_shape=jax.ShapeDtypeStruct((r, n), BF16),
        compiler_params=_params("arbitrary", "arbitrary"),
        name="moe_up",
    )(tile_expert, n_active, hg, wg, wu)


def _moe_down_kernel(te_ref, na_ref, a_ref, w_ref, o_ref, wbf):
    _expert_cast(te_ref, (w_ref,), (wbf,))

    @pl.when(pl.program_id(1) < na_ref[0])
    def _():
        acc = _dot(a_ref[...], wbf[...])
        half = acc.shape[1] // 2
        o_ref[...] = _pack_bf16_pair(acc[:, :half], acc[:, half:])

    @pl.when(pl.program_id(1) >= na_ref[0])
    def _():
        o_ref[...] = jnp.zeros_like(o_ref)


def moe_down(act, tile_expert, n_active, wd, *, tile=MOE_TILE, tn=MOE_DOWN_TN):
    r, k = act.shape
    n = wd.shape[2]
    return pl.pallas_call(
        _moe_down_kernel,
        grid_spec=pltpu.PrefetchScalarGridSpec(
            num_scalar_prefetch=2,
            grid=(n // tn, r // tile),
            in_specs=[pl.BlockSpec((tile, k), lambda j, i, te, na: (i, 0)),
                      pl.BlockSpec((1, k, tn), lambda j, i, te, na: (te[i], 0, j))],
            out_specs=pl.BlockSpec((tile, tn // 2), lambda j, i, te, na: (i, j)),
            scratch_shapes=[pltpu.VMEM((k, tn), BF16)]),
        out_shape=jax.ShapeDtypeStruct((r, n // 2), jnp.uint32),
        compiler_params=_params("arbitrary", "arbitrary"),
        name="moe_down",
    )(tile_expert, n_active, act, wd)


def _combine_kernel(pos_ref, og_hbm, x_ref, gate_ref, mod_ref, fg_ref, o_ref, buf1, buf2, sems, *, tm, pack):
    base = pl.program_id(0) * tm

    def copies(r):
        t = base + r
        return (pltpu.make_async_copy(og_hbm.at[pl.ds(pos_ref[2 * t], 1)], buf1.at[pl.ds(r, 1)], sems.at[0]),
                pltpu.make_async_copy(og_hbm.at[pl.ds(pos_ref[2 * t + 1], 1)], buf2.at[pl.ds(r, 1)], sems.at[1]))

    def issue(r, c):
        c1, c2 = copies(r)
        c1.start()
        c2.start()
        return c

    lax.fori_loop(0, tm, issue, 0)

    def drain(r, c):
        c1, c2 = copies(r)
        c1.wait()
        c2.wait()
        return c

    lax.fori_loop(0, tm, drain, 0)
    g = gate_ref[...]
    hi1, lo1 = _unpack_bf16_pair(buf1[...])
    hi2, lo2 = _unpack_bf16_pair(buf2[...])
    m_hi = g[:, 0:1] * hi1 + g[:, 1:2] * hi2
    m_lo = g[:, 0:1] * lo1 + g[:, 1:2] * lo2
    pieces = []
    for c in range(0, m_hi.shape[1], pack):
        pieces += [m_hi[:, c:c + pack], m_lo[:, c:c + pack]]
    moe = jnp.concatenate(pieces, axis=1)
    xn = x_ref[...] + mod_ref[0, 2:3, :] * moe
    o_ref[...] = _rms(xn) * fg_ref[...]


def moe_combine_final(og, pos_flat, x, gates, mod, final_g, seq, *, tm=256, down_tn=MOE_DOWN_TN):
    t, d = x.shape
    per_b = seq // tm
    kern = functools.partial(_combine_kernel, tm=tm, pack=min(down_tn, d) // 2)
    return pl.pallas_call(
        kern,
        grid_spec=pltpu.PrefetchScalarGridSpec(
            num_scalar_prefetch=1,
            grid=(t // tm,),
            in_specs=[pl.BlockSpec(memory_space=pl.ANY),
                      pl.BlockSpec((tm, d), lambda i, pos: (i, 0)),
                      pl.BlockSpec((tm, LANES), lambda i, pos: (i, 0)),
                      pl.BlockSpec((1, 3, d), lambda i, pos: (i // per_b, 0, 0)),
                      pl.BlockSpec((1, d), lambda i, pos: (0, 0))],
            out_specs=pl.BlockSpec((tm, d), lambda i, pos: (i, 0)),
            scratch_shapes=[pltpu.VMEM((tm, d // 2), jnp.uint32), pltpu.VMEM((tm, d // 2), jnp.uint32),
                            pltpu.SemaphoreType.DMA((2,))]),
        out_shape=jax.ShapeDtypeStruct((t, d), F32),
        compiler_params=_params("arbitrary"),
        name="moe_combine",
    )(pos_flat, og, x, gates, mod, final_g)


def moe_plan(idx, n_tokens, *, tile=MOE_TILE):
    e1 = idx[:, 0]
    e2 = idx[:, 1]
    flat_e = jnp.stack([e1, e2], axis=1).reshape(-1)
    onehot = (flat_e[:, None] == jnp.arange(N_EXPERTS)[None, :]).astype(jnp.int32)
    rank = jnp.cumsum(onehot, axis=0) - onehot
    counts = jnp.sum(onehot, axis=0)
    padded = ((counts + tile - 1) // tile) * tile
    ends = jnp.cumsum(padded)
    starts = ends - padded
    pos = jnp.sum(onehot * (rank + starts[None, :]), axis=1)
    n_slots = 2 * n_tokens + N_EXPERTS * tile
    n_tiles = n_slots // tile
    tok = jnp.repeat(jnp.arange(n_tokens, dtype=jnp.int32), 2)
    tok_of_slot = jnp.zeros((n_slots,), jnp.int32).at[pos].set(tok)
    n_active = (ends[-1] // tile).astype(jnp.int32)
    tile_start = jnp.arange(n_tiles, dtype=jnp.int32) * tile
    tile_expert = jnp.sum((tile_start[:, None] >= ends[None, :]).astype(jnp.int32), axis=1)
    last_expert = tile_expert[jnp.maximum(n_active - 1, 0)]
    tile_expert = jnp.where(jnp.arange(n_tiles) < n_active, tile_expert, last_expert).astype(jnp.int32)
    return pos.astype(jnp.int32), tok_of_slot, tile_expert, n_active.reshape(1), n_slots


def kernel(x, c, mix0_norm_g, mix0_ada_w, mix0_ada_b, mix0_w_in, gm_ln_g, gm_w_s, gm_b_s, mix0_w_out, ffn0_norm_g, ffn0_ada_w, ffn0_ada_b, ffn0_w_gate, ffn0_w_up, ffn0_w_down, mix1_norm_g, mix1_ada_w, mix1_ada_b, ssm_w_in, ssm_lam_re, ssm_lam_im, ssm_log_dt, ssm_b_re, ssm_b_im, ssm_c_re, ssm_c_im, ssm_d, glu_w_a, glu_w_b, moe_norm_g, moe_ada_w, moe_ada_b, moe_w_router, moe_w_gate, moe_w_up, moe_w_down, final_norm_g):
    bsz, seq, d = x.shape
    t = bsz * seq
    n_heads = mix0_w_in.shape[2] // (5 * SB_HEAD_DIM)
    n_groups = gm_w_s.shape[1]
    xf = x.reshape(t, d)
    c_pad = jnp.zeros((8, d), F32).at[:bsz].set(c)
    tm_big = min(1024, seq)

    def ada(w, b):
        m = ada_params(c_pad, w[0], b[0][None, :])
        return m[:bsz].reshape(bsz, 3, d)

    mod = ada(mix0_ada_w, mix0_ada_b)
    h = norm_mod(xf, mix0_norm_g, mod, seq)
    qkvz = matmul(h, mix0_w_in[0], tm=tm_big,tn=512, out_dtype=BF16)
    a_out = stick_breaking_attention(qkvz, bsz, seq, n_heads)
    b_out = spatial_gating(qkvz, gm_ln_g[0], gm_w_s[0], gm_b_s[0], n_groups)
    xf = matmul2_resid(a_out, b_out, mix0_w_out[0], xf, mod, seq, tm=tm_big,tn=512)
    mod = ada(ffn0_ada_w, ffn0_ada_b)
    h = norm_mod(xf, ffn0_norm_g, mod, seq)
    act = matmul_swiglu(h, ffn0_w_gate[0], ffn0_w_up[0], tm=tm_big,tn=256)
    xf = matmul_resid(act, ffn0_w_down[0], xf, mod, seq, tm=256, tn=512)
    mod = ada(mix1_ada_w, mix1_ada_b)
    h = norm_mod(xf, mix1_norm_g, mod, seq)
    u = matmul(h, ssm_w_in[0], tm=tm_big,tn=512, out_dtype=F32)
    ops = ssm_operators(ssm_lam_re[0], ssm_lam_im[0], ssm_log_dt[0], ssm_b_re[0], ssm_b_im[0],
                        ssm_c_re[0], ssm_c_im[0])
    y = ssm_mixer(u, ops, ssm_d, bsz)
    xf = matmul_glu_resid(y, glu_w_a[0], glu_w_b[0], xf, mod, seq, tm=tm_big,tn=256)
    mod = ada(moe_ada_w, moe_ada_b)
    w_router_pad = jnp.zeros((d, LANES), F32).at[:, :N_EXPERTS].set(moe_w_router[0])
    h32, logits = norm_router(xf, moe_norm_g, mod, w_router_pad, seq)
    idx, gates = route_top2(logits)
    pos, tok_of_slot, tile_expert, n_active, n_slots = moe_plan(idx, t)
    hg = moe_dispatch(h32, tok_of_slot, n_slots)
    act = moe_up(hg, tile_expert, n_active, moe_w_gate[0], moe_w_up[0])
    og = moe_down(act, tile_expert, n_active, moe_w_down[0])
    out = moe_combine_final(og, pos, xf, gates, mod, final_norm_g[None, :], seq)
    return out.reshape(bsz, seq, d)
```

```python
import functools
import math

import jax
import jax.numpy as jnp
from jax import lax
from jax.experimental import pallas as pl
from jax.experimental.pallas import tpu as pltpu

EPS = 1e-6
SB_HEAD_DIM = 128
SB_TILE = 256
GM_GROUP_DIM = 128
GM_CHUNK = 128
GM_ROWS = 2048
SSM_GROUP = 16
SSM_STATE = 64
SSM_CHUNK = 8
LANES = 128
N_EXPERTS = 8
MOE_TILE = 256
SB_SKIP_EXPONENT = 104.0
VMEM_LIMIT_BYTES = 56 * 1024 * 1024

BF16 = jnp.bfloat16
F32 = jnp.float32


def _params(*sem):
    return pltpu.CompilerParams(dimension_semantics=sem, vmem_limit_bytes=VMEM_LIMIT_BYTES)


def _dot(a, b):
    return jnp.dot(a, b, preferred_element_type=F32)


def _ada_kernel(c_ref, w_ref, b_ref, o_ref):
    c = c_ref[...]
    s = (c * jax.nn.sigmoid(c))
    s_hi = s.astype(BF16)
    s_lo = (s - s_hi.astype(F32)).astype(BF16)
    w = w_ref[...].astype(BF16)
    o_ref[...] = _dot(s_hi, w) + _dot(s_lo, w) + b_ref[...]


def ada_params(c_pad, w, b, *, tn=512):
    rows, d = c_pad.shape
    n = w.shape[1]
    return pl.pallas_call(
        _ada_kernel,
        grid=(n // tn,),
        in_specs=[pl.BlockSpec((rows, d), lambda j: (0, 0)),
                  pl.BlockSpec((d, tn), lambda j: (0, j)),
                  pl.BlockSpec((1, tn), lambda j: (0, j))],
        out_specs=pl.BlockSpec((rows, tn), lambda j: (0, j)),
        out_shape=jax.ShapeDtypeStruct((rows, n), F32),
        compiler_params=_params("arbitrary"),
        name="ada_params",
    )(c_pad, w, b)


def _rms(x):
    return x * lax.rsqrt(jnp.mean(x * x, axis=-1, keepdims=True) + EPS)


def _norm_mod_kernel(x_ref, g_ref, mod_ref, o_ref):
    y = _rms(x_ref[...]) * g_ref[...]
    o_ref[...] = (y * (1.0 + mod_ref[0, 1:2, :]) + mod_ref[0, 0:1, :]).astype(o_ref.dtype)


def norm_mod(x, g, mod, seq, *, tm=512, out_dtype=BF16):
    t, d = x.shape
    per_b = seq // tm
    return pl.pallas_call(
        _norm_mod_kernel,
        grid=(t // tm,),
        in_specs=[pl.BlockSpec((tm, d), lambda i: (i, 0)),
                  pl.BlockSpec((1, d), lambda i: (0, 0)),
                  pl.BlockSpec((1, 3, d), lambda i: (i // per_b, 0, 0))],
        out_specs=pl.BlockSpec((tm, d), lambda i: (i, 0)),
        out_shape=jax.ShapeDtypeStruct((t, d), out_dtype),
        compiler_params=_params("arbitrary"),
        name="norm_mod",
    )(x, g, mod)


def _norm_router_kernel(x_ref, g_ref, mod_ref, wr_ref, h_ref, logit_ref):
    y = _rms(x_ref[...]) * g_ref[...]
    h = y * (1.0 + mod_ref[0, 1:2, :]) + mod_ref[0, 0:1, :]
    h_ref[...] = h
    logit_ref[...] = jnp.dot(h, wr_ref[...], preferred_element_type=F32, precision=lax.Precision.HIGHEST)


def norm_router(x, g, mod, w_router_pad, seq, *, tm=512):
    t, d = x.shape
    per_b = seq // tm
    return pl.pallas_call(
        _norm_router_kernel,
        grid=(t // tm,),
        in_specs=[pl.BlockSpec((tm, d), lambda i: (i, 0)),
                  pl.BlockSpec((1, d), lambda i: (0, 0)),
                  pl.BlockSpec((1, 3, d), lambda i: (i // per_b, 0, 0)),
                  pl.BlockSpec((d, LANES), lambda i: (0, 0))],
        out_specs=[pl.BlockSpec((tm, d), lambda i: (i, 0)),
                   pl.BlockSpec((tm, LANES), lambda i: (i, 0))],
        out_shape=[jax.ShapeDtypeStruct((t, d), F32), jax.ShapeDtypeStruct((t, LANES), F32)],
        compiler_params=_params("arbitrary"),
        name="norm_router",
    )(x, g, mod, w_router_pad)


def _final_norm_kernel(x_ref, g_ref, o_ref):
    o_ref[...] = _rms(x_ref[...]) * g_ref[...]


def final_norm(x, g, *, tm=256):
    t, d = x.shape
    return pl.pallas_call(
        _final_norm_kernel,
        grid=(t // tm,),
        in_specs=[pl.BlockSpec((tm, d), lambda i: (i, 0)), pl.BlockSpec((1, d), lambda i: (0, 0))],
        out_specs=pl.BlockSpec((tm, d), lambda i: (i, 0)),
        out_shape=jax.ShapeDtypeStruct((t, d), F32),
        compiler_params=_params("arbitrary"),
    )(x, g)


def _cast_once(w_ref, wbf_ref):
    @pl.when(pl.program_id(1) == 0)
    def _():
        wbf_ref[...] = w_ref[...].astype(BF16)


def _mm_kernel(a_ref, w_ref, o_ref, wbf):
    _cast_once(w_ref, wbf)
    o_ref[...] = _dot(a_ref[...], wbf[...]).astype(o_ref.dtype)


def matmul(a, w, *, tm, tn, out_dtype):
    m, k = a.shape
    n = w.shape[1]
    return pl.pallas_call(
        _mm_kernel,
        grid=(n // tn, m // tm),
        in_specs=[pl.BlockSpec((tm, k), lambda j, i: (i, 0)),
                  pl.BlockSpec((k, tn), lambda j, i: (0, j))],
        out_specs=pl.BlockSpec((tm, tn), lambda j, i: (i, j)),
        out_shape=jax.ShapeDtypeStruct((m, n), out_dtype),
        scratch_shapes=[pltpu.VMEM((k, tn), BF16)],
        compiler_params=_params("arbitrary", "arbitrary"),
        name="mm",
    )(a, w)


def _mm_resid_kernel(a_ref, w_ref, x_ref, mod_ref, o_ref, wbf):
    _cast_once(w_ref, wbf)
    o_ref[...] = x_ref[...] + mod_ref[0, 2:3, :] * _dot(a_ref[...], wbf[...])


def matmul_resid(a, w, x, mod, seq, *, tm, tn):
    m, k = a.shape
    n = w.shape[1]
    per_b = seq // tm
    return pl.pallas_call(
        _mm_resid_kernel,
        grid=(n // tn, m // tm),
        in_specs=[pl.BlockSpec((tm, k), lambda j, i: (i, 0)),
                  pl.BlockSpec((k, tn), lambda j, i: (0, j), pipeline_mode=pl.Buffered(1)),
                  pl.BlockSpec((tm, tn), lambda j, i: (i, j)),
                  pl.BlockSpec((1, 3, tn), lambda j, i: (i // per_b, 0, j))],
        out_specs=pl.BlockSpec((tm, tn), lambda j, i: (i, j)),
        out_shape=jax.ShapeDtypeStruct((m, n), F32),
        scratch_shapes=[pltpu.VMEM((k, tn), BF16)],
        compiler_params=_params("arbitrary", "arbitrary"),
        name="mm_resid",
    )(a, w, x, mod)


def _mm2_resid_kernel(a_ref, b_ref, w_ref, x_ref, mod_ref, o_ref, wbf):
    _cast_once(w_ref, wbf)
    ka = a_ref.shape[1]
    acc = _dot(a_ref[...], wbf[:ka, :]) + _dot(b_ref[...], wbf[ka:, :])
    o_ref[...] = x_ref[...] + mod_ref[0, 2:3, :] * acc


def matmul2_resid(a, b, w, x, mod, seq, *, tm, tn):
    m, ka = a.shape
    kb = b.shape[1]
    n = w.shape[1]
    per_b = seq // tm
    return pl.pallas_call(
        _mm2_resid_kernel,
        grid=(n // tn, m // tm),
        in_specs=[pl.BlockSpec((tm, ka), lambda j, i: (i, 0)),
                  pl.BlockSpec((tm, kb), lambda j, i: (i, 0)),
                  pl.BlockSpec((ka + kb, tn), lambda j, i: (0, j)),
                  pl.BlockSpec((tm, tn), lambda j, i: (i, j)),
                  pl.BlockSpec((1, 3, tn), lambda j, i: (i // per_b, 0, j))],
        out_specs=pl.BlockSpec((tm, tn), lambda j, i: (i, j)),
        out_shape=jax.ShapeDtypeStruct((m, n), F32),
        scratch_shapes=[pltpu.VMEM((ka + kb, tn), BF16)],
        compiler_params=_params("arbitrary", "arbitrary"),
        name="mm2_resid",
    )(a, b, w, x, mod)


def _cast2_once(w1_ref, w2_ref, w1bf, w2bf):
    @pl.when(pl.program_id(1) == 0)
    def _():
        w1bf[...] = w1_ref[...].astype(BF16)
        w2bf[...] = w2_ref[...].astype(BF16)


def _mm_swiglu_kernel(a_ref, wg_ref, wu_ref, o_ref, wgbf, wubf):
    _cast2_once(wg_ref, wu_ref, wgbf, wubf)
    a = a_ref[...]
    g = _dot(a, wgbf[...])
    u = _dot(a, wubf[...])
    o_ref[...] = (g * jax.nn.sigmoid(g) * u).astype(o_ref.dtype)


def matmul_swiglu(a, wg, wu, *, tm, tn):
    m, k = a.shape
    n = wg.shape[1]
    return pl.pallas_call(
        _mm_swiglu_kernel,
        grid=(n // tn, m // tm),
        in_specs=[pl.BlockSpec((tm, k), lambda j, i: (i, 0)),
                  pl.BlockSpec((k, tn), lambda j, i: (0, j)),
                  pl.BlockSpec((k, tn), lambda j, i: (0, j))],
        out_specs=pl.BlockSpec((tm, tn), lambda j, i: (i, j)),
        out_shape=jax.ShapeDtypeStruct((m, n), BF16),
        scratch_shapes=[pltpu.VMEM((k, tn), BF16), pltpu.VMEM((k, tn), BF16)],
        compiler_params=_params("arbitrary", "arbitrary"),
        name="mm_swiglu",
    )(a, wg, wu)


def _mm_glu_resid_kernel(a_ref, wa_ref, wb_ref, x_ref, mod_ref, o_ref, wabf, wbbf):
    _cast2_once(wa_ref, wb_ref, wabf, wbbf)
    a = a_ref[...]
    p = _dot(a, wabf[...])
    q = _dot(a, wbbf[...])
    o_ref[...] = x_ref[...] + mod_ref[0, 2:3, :] * (p * jax.nn.sigmoid(q))


def matmul_glu_resid(a, wa, wb, x, mod, seq, *, tm, tn):
    m, k = a.shape
    n = wa.shape[1]
    per_b = seq // tm
    return pl.pallas_call(
        _mm_glu_resid_kernel,
        grid=(n // tn, m // tm),
        in_specs=[pl.BlockSpec((tm, k), lambda j, i: (i, 0)),
                  pl.BlockSpec((k, tn), lambda j, i: (0, j)),
                  pl.BlockSpec((k, tn), lambda j, i: (0, j)),
                  pl.BlockSpec((tm, tn), lambda j, i: (i, j)),
                  pl.BlockSpec((1, 3, tn), lambda j, i: (i // per_b, 0, j))],
        out_specs=pl.BlockSpec((tm, tn), lambda j, i: (i, j)),
        out_shape=jax.ShapeDtypeStruct((m, n), F32),
        scratch_shapes=[pltpu.VMEM((k, tn), BF16), pltpu.VMEM((k, tn), BF16)],
        compiler_params=_params("arbitrary", "arbitrary"),
        name="mm_glu",
    )(a, wa, wb, x, mod)


def _sb_kernel(q_ref, k_ref, v_ref, o_ref, *, tile, scale, heads):
    qi = pl.program_id(2)
    dh = q_ref.shape[1] // heads
    row = lax.broadcasted_iota(jnp.int32, (tile, tile), 0)
    col = lax.broadcasted_iota(jnp.int32, (tile, tile), 1)
    revcum = jnp.where(row >= col, 1.0, 0.0).astype(BF16)

    def cond(carry):
        step, _, min_later = carry
        return jnp.logical_and(step <= qi, min_later <= SB_SKIP_EXPONENT)

    def body(carry):
        step, state, _ = carry
        kj = qi - step
        start = pl.multiple_of(kj * tile, tile)
        past = (col + kj * tile) < (row + qi * tile)
        new_state = []
        min_later = None
        for h in range(heads):
            acc, later = state[h]
            lanes = slice(h * dh, (h + 1) * dh)
            kb = k_ref[pl.ds(start, tile), lanes]
            vb = v_ref[pl.ds(start, tile), lanes]
            z = lax.dot_general(q_ref[:, lanes], kb, (((1,), (1,)), ((), ())), preferred_element_type=F32) * scale
            sp = jnp.where(past, jnp.maximum(z, 0.0) + jnp.log1p(jnp.exp(-jnp.abs(z))), 0.0)
            sp_hi = sp.astype(BF16)
            sp_lo = (sp - sp_hi.astype(F32)).astype(BF16)
            cum = _dot(sp_hi, revcum) + _dot(sp_lo, revcum)
            w = jnp.where(past, jnp.exp(z - cum - later), 0.0)
            acc = acc + _dot(w.astype(BF16), vb)
            later = later + jnp.sum(sp, axis=1, keepdims=True)
            new_state.append((acc, later))
            m = jnp.min(later)
            min_later = m if min_later is None else jnp.minimum(min_later, m)
        return step + 1, tuple(new_state), min_later

    state0 = tuple((jnp.zeros((tile, dh), F32), jnp.zeros((tile, 1), F32)) for _ in range(heads))
    _, state, _ = lax.while_loop(cond, body, (jnp.int32(0), state0, jnp.float32(0.0)))
    for h in range(heads):
        o_ref[:, h * dh:(h + 1) * dh] = state[h][0].astype(o_ref.dtype)


def stick_breaking_attention(qkvz, bsz, seq, n_heads, *, tile=SB_TILE, heads=2):
    t = qkvz.shape[0]
    dh = SB_HEAD_DIM
    nq = seq // tile
    hg = n_heads // heads
    kern = functools.partial(_sb_kernel, tile=tile, scale=1.0 / math.sqrt(dh), heads=heads)
    return pl.pallas_call(
        kern,
        grid=(bsz, hg, nq),
        in_specs=[pl.BlockSpec((tile, heads * dh), lambda b, h, i: (b * nq + i, h)),
                  pl.BlockSpec((seq, heads * dh), lambda b, h, i: (b, hg + h)),
                  pl.BlockSpec((seq, heads * dh), lambda b, h, i: (b, 2 * hg + h))],
        out_specs=pl.BlockSpec((tile, heads * dh), lambda b, h, i: (b * nq + i, h)),
        out_shape=jax.ShapeDtypeStruct((t, n_heads * dh), BF16),
        compiler_params=_params("arbitrary", "arbitrary", "arbitrary"),
        name="sb_attention",
    )(qkvz, qkvz, qkvz)


def _gelu(x):
    return jax.nn.gelu(x, approximate=True)


def _gm_kernel(z1_ref, z2_ref, lng_ref, ws_ref, bs_ref, o_ref, *, chunk):
    u = _gelu(z1_ref[...].astype(F32))
    v = _gelu(z2_ref[...].astype(F32))
    mu = jnp.mean(v, axis=-1, keepdims=True)
    vc = v - mu
    var = jnp.mean(vc * vc, axis=-1, keepdims=True)
    vn = vc * lax.rsqrt(var + EPS) * lng_ref[0]
    row = lax.broadcasted_iota(jnp.int32, (chunk, chunk), 0)
    col = lax.broadcasted_iota(jnp.int32, (chunk, chunk), 1)
    w = jnp.where(row >= col, ws_ref[0], 0.0)
    bias = bs_ref[0]
    for ci in range(u.shape[0] // chunk):
        sl = slice(ci * chunk, (ci + 1) * chunk)
        mixed = jnp.dot(w, vn[sl], preferred_element_type=F32, precision=lax.Precision.HIGHEST) + bias
        o_ref[sl, :] = (u[sl] * mixed).astype(o_ref.dtype)


def spatial_gating(qkvz, ln_g, w_s, b_s, n_groups, *, rows=GM_ROWS):
    t = qkvz.shape[0]
    c = GM_GROUP_DIM
    rows = min(rows, t)
    kern = functools.partial(_gm_kernel, chunk=GM_CHUNK)
    return pl.pallas_call(
        kern,
        grid=(n_groups, t // rows),
        in_specs=[pl.BlockSpec((rows, c), lambda g, i: (i, 3 * n_groups + g)),
                  pl.BlockSpec((rows, c), lambda g, i: (i, 4 * n_groups + g)),
                  pl.BlockSpec((1, 1, c), lambda g, i: (g, 0, 0)),
                  pl.BlockSpec((1, GM_CHUNK, GM_CHUNK), lambda g, i: (g, 0, 0)),
                  pl.BlockSpec((1, GM_CHUNK, 1), lambda g, i: (g, 0, 0))],
        out_specs=pl.BlockSpec((rows, c), lambda g, i: (i, g)),
        out_shape=jax.ShapeDtypeStruct((t, n_groups * c), BF16),
        compiler_params=_params("arbitrary", "arbitrary"),
        name="spatial_gating",
    )(qkvz, qkvz, ln_g[:, None, :], w_s, b_s[:, :, None])


def ssm_operators(lam_re, lam_im, log_dt, b_re, b_im, c_re, c_im):
    g_all, p = lam_re.shape
    cg = SSM_GROUP
    L = SSM_CHUNK
    gpt = LANES // cg
    nt = g_all // gpt
    lr = jnp.minimum(lam_re, -1e-4)
    li = lam_im
    dt = jnp.exp(log_dt)[:, None]
    mag = jnp.exp(lr * dt)
    a_re = mag * jnp.cos(li * dt)
    a_im = mag * jnp.sin(li * dt)
    den = lr * lr + li * li
    nr = a_re - 1.0
    f_re = (nr * lr + a_im * li) / den
    f_im = (a_im * lr - nr * li) / den
    bb_re = f_re[:, :, None] * b_re - f_im[:, :, None] * b_im
    bb_im = f_re[:, :, None] * b_im + f_im[:, :, None] * b_re
    tau = jnp.arange(L + 1, dtype=F32)[:, None, None]
    pmag = jnp.exp(lr * dt * tau)
    pw_re = pmag * jnp.cos(li * dt * tau)
    pw_im = pmag * jnp.sin(li * dt * tau)
    ab_re = pw_re[..., None] * bb_re - pw_im[..., None] * bb_im
    ab_im = pw_re[..., None] * bb_im + pw_im[..., None] * bb_re
    kk = (jnp.einsum('gcp,lgpd->lgcd', c_re, ab_re[:L], precision=lax.Precision.HIGHEST)
          - jnp.einsum('gcp,lgpd->lgcd', c_im, ab_im[:L], precision=lax.Precision.HIGHEST))
    eye = jnp.eye(gpt, dtype=F32)
    k_c = jnp.einsum('ljgcd,gh->jlgdhc', kk.reshape(L, nt, gpt, cg, cg), eye).reshape(nt, L, LANES, LANES)
    def state_in(ab):
        v = ab[:L][::-1].reshape(L, nt, gpt, p, cg).transpose(1, 0, 2, 4, 3).reshape(nt, L, LANES, p)
        return jnp.concatenate([v] * (LANES // p), axis=-1)
    b_c = jnp.stack([state_in(ab_re), state_in(ab_im)], axis=2)
    q_re = pw_re[1:]
    q_im = pw_im[1:]
    co_re = c_re[None] * q_re[:, :, None, :] - c_im[None] * q_im[:, :, None, :]
    co_im = -(c_re[None] * q_im[:, :, None, :] + c_im[None] * q_re[:, :, None, :])
    def state_out(co):
        v = co.reshape(L, nt, gpt, cg, p).transpose(1, 0, 4, 2, 3).reshape(nt, L, p, LANES)
        return jnp.concatenate([v] * (LANES // p), axis=-2)
    c_c = jnp.stack([state_out(co_re), state_out(co_im)], axis=2)
    al_re = pw_re[L].reshape(nt, 1, gpt * p)
    al_im = pw_im[L].reshape(nt, 1, gpt * p)
    return k_c.astype(BF16), b_c.astype(BF16), c_c.astype(BF16), al_re, al_im


def _ssm_expand(kc_ref, bc_ref, cc_ref, m_s, b_s, c_s):
    L = kc_ref.shape[1]
    nstate_half = b_s.shape[1] // 2
    pairs = nstate_half // LANES
    row = lax.broadcasted_iota(jnp.int32, (LANES, LANES), 0)
    lane = lax.broadcasted_iota(jnp.int32, (LANES, LANES), 1)
    per_blk = LANES // SSM_STATE
    zero = jnp.zeros((LANES, LANES), BF16)

    @pl.when(pl.program_id(0) == 0)
    def _():
        m_s[...] = jnp.zeros_like(m_s)

    for s_in in range(L):
        for s_out in range(s_in, L):
            m_s[s_in * LANES:(s_in + 1) * LANES, s_out * LANES:(s_out + 1) * LANES] = kc_ref[0, s_out - s_in]
    for pi in range(pairs):
        in_mask = (row // SSM_GROUP) == (per_blk * pi + lane // SSM_STATE)
        out_mask = (lane // SSM_GROUP) == (per_blk * pi + row // SSM_STATE)
        for s in range(L):
            for ri in range(2):
                col = ri * nstate_half + pi * LANES
                b_s[s * LANES:(s + 1) * LANES, col:col + LANES] = jnp.where(in_mask, bc_ref[0, s, ri], zero)
                c_s[col:col + LANES, s * LANES:(s + 1) * LANES] = jnp.where(out_mask, cc_ref[0, s, ri], zero)


def _ssm_kernel(u_ref, kc_ref, bc_ref, cc_ref, are_ref, aim_ref, d_ref, o_ref,
                m_s, b_s, c_s, xl3, xp3, pw3, ybuf, *, bsz, nseg, seglen, pitch):
    L = kc_ref.shape[1]
    nblk = xl3.shape[0]
    hb = nblk // 2
    nseq = bsz * nseg
    _ssm_expand(kc_ref, bc_ref, cc_ref, m_s, b_s, c_s)

    @pl.when(pl.program_id(0) == 0)
    def _():
        xp3[...] = jnp.zeros_like(xp3)

    zpad = jnp.zeros((pitch - seglen, LANES), F32)
    u_steps = []
    for s in range(L):
        pieces = []
        for m in range(nseq):
            pieces += [u_ref[pl.ds(m * seglen * L + s, seglen, stride=L), :], zpad]
        u_steps.append(jnp.concatenate(pieces, axis=0))
    u2 = jnp.concatenate(u_steps, axis=1).astype(BF16)
    xl = _dot(u2, b_s[...])
    for kb in range(nblk):
        xl3[kb] = xl[:, kb * LANES:(kb + 1) * LANES]
    a_re = [are_ref[0, :, kb * LANES:(kb + 1) * LANES] for kb in range(hb)]
    a_im = [aim_ref[0, :, kb * LANES:(kb + 1) * LANES] for kb in range(hb)]

    seg_pow = []
    for kb in range(hb):
        pw3[kb, 0:1, :] = jnp.ones((1, LANES), F32)
        pw3[hb + kb, 0:1, :] = jnp.zeros((1, LANES), F32)
        qr, qi = a_re[kb], a_im[kb]
        n = 1
        while n < seglen:
            pr, pi = pw3[kb, 0:n, :], pw3[hb + kb, 0:n, :]
            pw3[kb, n:2 * n, :] = pr * qr - pi * qi
            pw3[hb + kb, n:2 * n, :] = pr * qi + pi * qr
            qr, qi = qr * qr - qi * qi, 2.0 * qr * qi
            n *= 2
        seg_pow.append((qr, qi))

    def step(k, xs):
        rows = pl.ds(k, nseq, stride=pitch)
        new_x = []
        for kb in range(hb):
            xr, xi = xs[kb]
            xp3.at[kb][rows, :] = xr
            xp3.at[hb + kb][rows, :] = xi
            lr = xl3.at[kb][rows, :]
            li = xl3.at[hb + kb][rows, :]
            new_x.append((a_re[kb] * xr - a_im[kb] * xi + lr, a_re[kb] * xi + a_im[kb] * xr + li))
        return tuple(new_x)

    x0 = tuple((jnp.zeros((nseq, LANES), F32), jnp.zeros((nseq, LANES), F32)) for _ in range(hb))
    xs = lax.fori_loop(0, seglen, step, x0)

    for kb in range(hb):
        pr = pw3[kb]
        pi = pw3[hb + kb]
        fr, fi = xs[kb]
        qr, qi = seg_pow[kb]
        for b in range(bsz):
            sr = si = None
            for q in range(1, nseg):
                m = b * nseg + q
                er, ei = fr[m - 1:m], fi[m - 1:m]
                if sr is None:
                    sr, si = er, ei
                else:
                    sr, si = er + qr * sr - qi * si, ei + qr * si + qi * sr
                sl = slice(m * pitch, m * pitch + seglen)
                xp3[kb, sl, :] = xp3[kb, sl, :] + (pr * sr - pi * si)
                xp3[hb + kb, sl, :] = xp3[hb + kb, sl, :] + (pr * si + pi * sr)

    xp = jnp.concatenate([xp3[kb] for kb in range(nblk)], axis=1).astype(BF16)
    y = _dot(u2, m_s[...]) + _dot(xp, c_s[...])
    d = d_ref[...]
    for s in range(L):
        ys = _gelu(y[:, s * LANES:(s + 1) * LANES] + d * u_steps[s])
        for m in range(nseq):
            ybuf[pl.ds(m * seglen * L + s, seglen, stride=L), :] = ys[m * pitch:m * pitch + seglen]
    o_ref[...] = ybuf[...].astype(o_ref.dtype)


def ssm_mixer(u, ops, d_skip, bsz):
    k_c, b_c, c_c, al_re, al_im = ops
    t, width = u.shape
    L = k_c.shape[1]
    nrow = t // L
    nt = width // LANES
    nstate = 2 * al_re.shape[2]
    sublanes = 8
    nseg = sublanes // bsz
    seglen = nrow // (bsz * nseg)
    nblk = nstate // LANES
    assert seglen & (seglen - 1) == 0 and nrow == bsz * nseg * seglen
    pitch = seglen + sublanes
    prow = bsz * nseg * pitch
    kern = functools.partial(_ssm_kernel, bsz=bsz, nseg=nseg, seglen=seglen, pitch=pitch)
    return pl.pallas_call(
        kern,
        grid=(nt,),
        in_specs=[pl.BlockSpec((t, LANES), lambda j: (0, j)),
                  pl.BlockSpec((1, L, LANES, LANES), lambda j: (j, 0, 0, 0)),
                  pl.BlockSpec((1, L, 2, LANES, LANES), lambda j: (j, 0, 0, 0, 0)),
                  pl.BlockSpec((1, L, 2, LANES, LANES), lambda j: (j, 0, 0, 0, 0)),
                  pl.BlockSpec((1, 1, nstate // 2), lambda j: (j, 0, 0)),
                  pl.BlockSpec((1, 1, nstate // 2), lambda j: (j, 0, 0)),
                  pl.BlockSpec((1, LANES), lambda j: (0, j))],
        out_specs=pl.BlockSpec((t, LANES), lambda j: (0, j)),
        out_shape=jax.ShapeDtypeStruct((t, width), BF16),
        scratch_shapes=[pltpu.VMEM((L * LANES, L * LANES), BF16), pltpu.VMEM((L * LANES, nstate), BF16),
                        pltpu.VMEM((nstate, L * LANES), BF16),
                        pltpu.VMEM((nblk, prow, LANES), F32), pltpu.VMEM((nblk, prow, LANES), F32),
                        pltpu.VMEM((nblk, seglen, LANES), F32), pltpu.VMEM((t, LANES), F32)],
        compiler_params=_params("arbitrary"),
        name="ssm_mixer",
    )(u, k_c, b_c, c_c, al_re, al_im, d_skip)


def _route_kernel(logit_ref, idx_ref, gate_ref, *, n_experts):
    lg = logit_ref[...]
    lane = lax.broadcasted_iota(jnp.int32, lg.shape, 1)
    neg = jnp.float32(-jnp.inf)
    lg = jnp.where(lane < n_experts, lg, neg)
    m1 = jnp.max(lg, axis=-1, keepdims=True)
    i1 = jnp.min(jnp.where(lg == m1, lane, LANES), axis=-1, keepdims=True)
    lg2 = jnp.where(lane == i1, neg, lg)
    m2 = jnp.max(lg2, axis=-1, keepdims=True)
    i2 = jnp.min(jnp.where(lg2 == m2, lane, LANES), axis=-1, keepdims=True)
    e2 = jnp.exp(m2 - m1)
    p1 = 1.0 / (1.0 + e2)
    p2 = e2 / (1.0 + e2)
    idx_ref[...] = jnp.where(lane == 0, i1, jnp.where(lane == 1, i2, 0))
    gate_ref[...] = jnp.where(lane == 0, p1, jnp.where(lane == 1, p2, 0.0))


def route_top2(logits, *, tm=512):
    t = logits.shape[0]
    kern = functools.partial(_route_kernel, n_experts=N_EXPERTS)
    return pl.pallas_call(
        kern,
        grid=(t // tm,),
        in_specs=[pl.BlockSpec((tm, LANES), lambda i: (i, 0))],
        out_specs=[pl.BlockSpec((tm, LANES), lambda i: (i, 0)), pl.BlockSpec((tm, LANES), lambda i: (i, 0))],
        out_shape=[jax.ShapeDtypeStruct((t, LANES), jnp.int32), jax.ShapeDtypeStruct((t, LANES), F32)],
        compiler_params=_params("arbitrary"),
    )(logits)


def _dispatch_kernel(tok_ref, h_hbm, o_ref, buf, sems, *, tile):
    i = pl.program_id(0)

    def row_copy(step, r):
        slot = step % 2
        return pltpu.make_async_copy(h_hbm.at[pl.ds(tok_ref[step * tile + r], 1)],
                                     buf.at[slot, pl.ds(r, 1)], sems.at[slot])

    def issue_step(step):
        def issue(r, c):
            row_copy(step, r).start()
            return c
        lax.fori_loop(0, tile, issue, 0)

    @pl.when(i == 0)
    def _():
        issue_step(i)

    @pl.when(i + 1 < pl.num_programs(0))
    def _():
        issue_step(i + 1)

    def drain(r, c):
        row_copy(i, r).wait()
        return c

    lax.fori_loop(0, tile, drain, 0)
    o_ref[...] = buf[i % 2].astype(o_ref.dtype)


def moe_dispatch(h, tok_of_slot, n_slots, *, tile=MOE_TILE):
    d = h.shape[1]
    kern = functools.partial(_dispatch_kernel, tile=tile)
    return pl.pallas_call(
        kern,
        grid_spec=pltpu.PrefetchScalarGridSpec(
            num_scalar_prefetch=1,
            grid=(n_slots // tile,),
            in_specs=[pl.BlockSpec(memory_space=pl.ANY)],
            out_specs=pl.BlockSpec((tile, d), lambda i, tok: (i, 0)),
            scratch_shapes=[pltpu.VMEM((2, tile, d), F32), pltpu.SemaphoreType.DMA((2,))]),
        out_shape=jax.ShapeDtypeStruct((n_slots, d), BF16),
        compiler_params=_params("arbitrary"),
        name="moe_dispatch",
    )(tok_of_slot, h)


def _stream_expert_weights(te_ref, nre_ref, w_hbms, stages, wbfs, sems, tn):
    j = pl.program_id(0)
    i = pl.program_id(1)
    e = te_ref[i]
    run_start = jnp.logical_or(i == 0, e != te_ref[jnp.maximum(i - 1, 0)])

    def copies(ex, jx):
        col = pl.multiple_of(jx * tn, tn)
        return [pltpu.make_async_copy(w.at[ex, :, pl.ds(col, tn)], st, sems.at[k])
                for k, (w, st) in enumerate(zip(w_hbms, stages))]

    @pl.when(run_start)
    def _():
        @pl.when(jnp.logical_and(j == 0, i == 0))
        def _():
            for cp in copies(e, j):
                cp.start()

        for cp in copies(e, j):
            cp.wait()
        for st, wb in zip(stages, wbfs):
            wb[...] = st[...].astype(BF16)
        nxt = nre_ref[i]
        last = nxt < 0
        e_next = jnp.where(last, te_ref[0], nxt)
        j_next = jnp.where(last, j + 1, j)

        @pl.when(jnp.logical_not(jnp.logical_and(last, j == pl.num_programs(0) - 1)))
        def _():
            for cp in copies(e_next, j_next):
                cp.start()


def _moe_up_kernel(te_ref, na_ref, nre_ref, a_ref, wg_hbm, wu_hbm, o_ref, stg, stu, wgbf, wubf, sems, *, tn):
    _stream_expert_weights(te_ref, nre_ref, (wg_hbm, wu_hbm), (stg, stu), (wgbf, wubf), sems, tn)

    @pl.when(pl.program_id(1) < na_ref[0])
    def _():
        a = a_ref[...]
        g = _dot(a, wgbf[...])
        u = _dot(a, wubf[...])
        o_ref[...] = (g * jax.nn.sigmoid(g) * u).astype(o_ref.dtype)

    @pl.when(pl.program_id(1) >= na_ref[0])
    def _():
        o_ref[...] = jnp.zeros_like(o_ref)


def moe_up(hg, plan, wg, wu, *, tile=MOE_TILE, tn=512):
    tile_expert, n_active, next_run_expert = plan
    r, k = hg.shape
    n = wg.shape[2]
    tn = min(tn, n)
    kern = functools.partial(_moe_up_kernel, tn=tn)
    return pl.pallas_call(
        kern,
        grid_spec=pltpu.PrefetchScalarGridSpec(
            num_scalar_prefetch=3,
            grid=(n // tn, r // tile),
            in_specs=[pl.BlockSpec((tile, k), lambda j, i, te, na, nre: (i, 0)),
                      pl.BlockSpec(memory_space=pl.ANY),
                      pl.BlockSpec(memory_space=pl.ANY)],
            out_specs=pl.BlockSpec((tile, tn), lambda j, i, te, na, nre: (i, j)),
            scratch_shapes=[pltpu.VMEM((k, tn), F32), pltpu.VMEM((k, tn), F32),
                            pltpu.VMEM((k, tn), BF16), pltpu.VMEM((k, tn), BF16),
                            pltpu.SemaphoreType.DMA((2,))]),
        out_shape=jax.ShapeDtypeStruct((r, n), BF16),
        compiler_params=_params("arbitrary", "arbitrary"),
        name="moe_up",
    )(tile_expert, n_active, next_run_expert, hg, wg, wu)


def _moe_down_kernel(te_ref, na_ref, nre_ref, a_ref, w_hbm, o_ref, stage, wbf, sems, *, tn):
    _stream_expert_weights(te_ref, nre_ref, (w_hbm,), (stage,), (wbf,), sems, tn)

    @pl.when(pl.program_id(1) < na_ref[0])
    def _():
        o_ref[...] = _dot(a_ref[...], wbf[...])

    @pl.when(pl.program_id(1) >= na_ref[0])
    def _():
        o_ref[...] = jnp.zeros_like(o_ref)


def moe_down(act, plan, wd, *, tile=MOE_TILE, tn=1024):
    tile_expert, n_active, next_run_expert = plan
    r, k = act.shape
    n = wd.shape[2]
    tn = min(tn, n)
    kern = functools.partial(_moe_down_kernel, tn=tn)
    return pl.pallas_call(
        kern,
        grid_spec=pltpu.PrefetchScalarGridSpec(
            num_scalar_prefetch=3,
            grid=(n // tn, r // tile),
            in_specs=[pl.BlockSpec((tile, k), lambda j, i, te, na, nre: (i, 0)),
                      pl.BlockSpec(memory_space=pl.ANY)],
            out_specs=pl.BlockSpec((tile, tn), lambda j, i, te, na, nre: (i, j)),
            scratch_shapes=[pltpu.VMEM((k, tn), F32), pltpu.VMEM((k, tn), BF16),
                            pltpu.SemaphoreType.DMA((1,))]),
        out_shape=jax.ShapeDtypeStruct((r, n), F32),
        compiler_params=_params("arbitrary", "arbitrary"),
        name="moe_down",
    )(tile_expert, n_active, next_run_expert, act, wd)


def _combine_kernel(pos_ref, og_hbm, x_ref, gate_ref, mod_ref, fg_ref, o_ref, buf, sems, *, tm):
    i = pl.program_id(0)

    def copies(step, r):
        slot = step % 2
        t = step * tm + r
        return tuple(pltpu.make_async_copy(og_hbm.at[pl.ds(pos_ref[2 * t + c], 1)],
                                           buf.at[slot, c, pl.ds(r, 1)], sems.at[slot, c]) for c in range(2))

    def issue_step(step):
        def issue(r, carry):
            for cp in copies(step, r):
                cp.start()
            return carry
        lax.fori_loop(0, tm, issue, 0)

    @pl.when(i == 0)
    def _():
        issue_step(i)

    @pl.when(i + 1 < pl.num_programs(0))
    def _():
        issue_step(i + 1)

    def drain(r, carry):
        for cp in copies(i, r):
            cp.wait()
        return carry

    lax.fori_loop(0, tm, drain, 0)
    g = gate_ref[...]
    slot = i % 2
    moe = g[:, 0:1] * buf[slot, 0] + g[:, 1:2] * buf[slot, 1]
    xn = x_ref[...] + mod_ref[0, 2:3, :] * moe
    o_ref[...] = _rms(xn) * fg_ref[...]


def moe_combine_final(og, pos_flat, x, gates, mod, final_g, seq, *, tm=128):
    t, d = x.shape
    per_b = seq // tm
    kern = functools.partial(_combine_kernel, tm=tm)
    return pl.pallas_call(
        kern,
        grid_spec=pltpu.PrefetchScalarGridSpec(
            num_scalar_prefetch=1,
            grid=(t // tm,),
            in_specs=[pl.BlockSpec(memory_space=pl.ANY),
                      pl.BlockSpec((tm, d), lambda i, pos: (i, 0)),
                      pl.BlockSpec((tm, LANES), lambda i, pos: (i, 0)),
                      pl.BlockSpec((1, 3, d), lambda i, pos: (i // per_b, 0, 0)),
                      pl.BlockSpec((1, d), lambda i, pos: (0, 0))],
            out_specs=pl.BlockSpec((tm, d), lambda i, pos: (i, 0)),
            scratch_shapes=[pltpu.VMEM((2, 2, tm, d), F32), pltpu.SemaphoreType.DMA((2, 2))]),
        out_shape=jax.ShapeDtypeStruct((t, d), F32),
        compiler_params=_params("arbitrary"),
        name="moe_combine",
    )(pos_flat, og, x, gates, mod, final_g)


def moe_plan(idx, n_tokens, *, tile=MOE_TILE):
    e1 = idx[:, 0]
    e2 = idx[:, 1]
    flat_e = jnp.stack([e1, e2], axis=1).reshape(-1)
    onehot = (flat_e[:, None] == jnp.arange(N_EXPERTS)[None, :]).astype(jnp.int32)
    rank = jnp.cumsum(onehot, axis=0) - onehot
    counts = jnp.sum(onehot, axis=0)
    padded = ((counts + tile - 1) // tile) * tile
    ends = jnp.cumsum(padded)
    starts = ends - padded
    pos = jnp.sum(onehot * (rank + starts[None, :]), axis=1)
    n_slots = 2 * n_tokens + N_EXPERTS * tile
    n_tiles = n_slots // tile
    tok = jnp.repeat(jnp.arange(n_tokens, dtype=jnp.int32), 2)
    tok_of_slot = jnp.zeros((n_slots,), jnp.int32).at[pos].set(tok)
    n_active = (ends[-1] // tile).astype(jnp.int32)
    tile_start = jnp.arange(n_tiles, dtype=jnp.int32) * tile
    tile_expert = jnp.sum((tile_start[:, None] >= ends[None, :]).astype(jnp.int32), axis=1)
    last_expert = tile_expert[jnp.maximum(n_active - 1, 0)]
    tile_expert = jnp.where(jnp.arange(n_tiles) < n_active, tile_expert, last_expert).astype(jnp.int32)
    ids = jnp.arange(N_EXPERTS)
    later_present = jnp.logical_and(ids[None, :] > ids[:, None], (counts > 0)[None, :])
    next_present = jnp.min(jnp.where(later_present, ids[None, :], N_EXPERTS), axis=1)
    next_present = jnp.where(next_present == N_EXPERTS, -1, next_present).astype(jnp.int32)
    next_run_expert = next_present[tile_expert]
    plan = (tile_expert, n_active.reshape(1), next_run_expert)
    return pos.astype(jnp.int32), tok_of_slot, plan, n_slots


def kernel(x, c, mix0_norm_g, mix0_ada_w, mix0_ada_b, mix0_w_in, gm_ln_g, gm_w_s, gm_b_s, mix0_w_out, ffn0_norm_g, ffn0_ada_w, ffn0_ada_b, ffn0_w_gate, ffn0_w_up, ffn0_w_down, mix1_norm_g, mix1_ada_w, mix1_ada_b, ssm_w_in, ssm_lam_re, ssm_lam_im, ssm_log_dt, ssm_b_re, ssm_b_im, ssm_c_re, ssm_c_im, ssm_d, glu_w_a, glu_w_b, moe_norm_g, moe_ada_w, moe_ada_b, moe_w_router, moe_w_gate, moe_w_up, moe_w_down, final_norm_g):
    bsz, seq, d = x.shape
    t = bsz * seq
    n_heads = mix0_w_in.shape[2] // (5 * SB_HEAD_DIM)
    n_groups = gm_w_s.shape[1]
    xf = x.reshape(t, d)
    c_pad = jnp.zeros((8, d), F32).at[:bsz].set(c)
    tm_big = min(1024, seq)

    def ada(w, b):
        m = ada_params(c_pad, w[0], b[0][None, :])
        return m[:bsz].reshape(bsz, 3, d)

    mod = ada(mix0_ada_w, mix0_ada_b)
    h = norm_mod(xf, mix0_norm_g, mod, seq)
    qkvz = matmul(h, mix0_w_in[0], tm=tm_big,tn=512, out_dtype=BF16)
    a_out = stick_breaking_attention(qkvz, bsz, seq, n_heads)
    b_out = spatial_gating(qkvz, gm_ln_g[0], gm_w_s[0], gm_b_s[0], n_groups)
    xf = matmul2_resid(a_out, b_out, mix0_w_out[0], xf, mod, seq, tm=tm_big,tn=512)
    mod = ada(ffn0_ada_w, ffn0_ada_b)
    h = norm_mod(xf, ffn0_norm_g, mod, seq)
    act = matmul_swiglu(h, ffn0_w_gate[0], ffn0_w_up[0], tm=tm_big,tn=256)
    xf = matmul_resid(act, ffn0_w_down[0], xf, mod, seq, tm=256, tn=512)
    mod = ada(mix1_ada_w, mix1_ada_b)
    h = norm_mod(xf, mix1_norm_g, mod, seq)
    u = matmul(h, ssm_w_in[0], tm=tm_big,tn=512, out_dtype=F32)
    ops = ssm_operators(ssm_lam_re[0], ssm_lam_im[0], ssm_log_dt[0], ssm_b_re[0], ssm_b_im[0],
                        ssm_c_re[0], ssm_c_im[0])
    y = ssm_mixer(u, ops, ssm_d, bsz)
    xf = matmul_glu_resid(y, glu_w_a[0], glu_w_b[0], xf, mod, seq, tm=tm_big,tn=256)
    mod = ada(moe_ada_w, moe_ada_b)
    w_router_pad = jnp.zeros((d, LANES), F32).at[:, :N_EXPERTS].set(moe_w_router[0])
    h32, logits = norm_router(xf, moe_norm_g, mod, w_router_pad, seq)
    idx, gates = route_top2(logits)
    pos, tok_of_slot, plan, n_slots = moe_plan(idx, t)
    hg = moe_dispatch(h32, tok_of_slot, n_slots)
    act = moe_up(hg, plan, moe_w_gate[0], moe_w_up[0])
    og = moe_down(act, plan, moe_w_down[0])
    out = moe_combine_final(og, pos, xf, gates, mod, final_norm_g[None, :], seq)
    return out.reshape(bsz, seq, d)
```

```python
import functools
import math

import jax
import jax.numpy as jnp
from jax import lax
from jax.experimental import pallas as pl
from jax.experimental.pallas import tpu as pltpu

EPS = 1e-6
SB_HEAD_DIM = 128
SB_TILE = 256
GM_GROUP_DIM = 128
GM_CHUNK = 128
GM_ROWS = 2048
SSM_GROUP = 16
SSM_STATE = 64
SSM_CHUNK = 8
LANES = 128
N_EXPERTS = 8
MOE_TILE = 256
SB_SKIP_EXPONENT = 104.0
SB_MASKED_SCORE = -1e30
VMEM_LIMIT_BYTES = 56 * 1024 * 1024

BF16 = jnp.bfloat16
F32 = jnp.float32


def _params(*sem):
    return pltpu.CompilerParams(dimension_semantics=sem, vmem_limit_bytes=VMEM_LIMIT_BYTES)


def _dot(a, b):
    return jnp.dot(a, b, preferred_element_type=F32)


def _ada_kernel(c_ref, w_ref, b_ref, o_ref):
    c = c_ref[...]
    s = (c * jax.nn.sigmoid(c))
    s_hi = s.astype(BF16)
    s_lo = (s - s_hi.astype(F32)).astype(BF16)
    w = w_ref[...].astype(BF16)
    o_ref[...] = _dot(s_hi, w) + _dot(s_lo, w) + b_ref[...]


def ada_params(c_pad, w, b, *, tn=512):
    rows, d = c_pad.shape
    n = w.shape[1]
    return pl.pallas_call(
        _ada_kernel,
        grid=(n // tn,),
        in_specs=[pl.BlockSpec((rows, d), lambda j: (0, 0)),
                  pl.BlockSpec((d, tn), lambda j: (0, j)),
                  pl.BlockSpec((1, tn), lambda j: (0, j))],
        out_specs=pl.BlockSpec((rows, tn), lambda j: (0, j)),
        out_shape=jax.ShapeDtypeStruct((rows, n), F32),
        compiler_params=_params("arbitrary"),
        name="ada_params",
    )(c_pad, w, b)


def _rms(x):
    return x * lax.rsqrt(jnp.mean(x * x, axis=-1, keepdims=True) + EPS)


def _norm_mod_kernel(x_ref, g_ref, mod_ref, o_ref):
    y = _rms(x_ref[...]) * g_ref[...]
    o_ref[...] = (y * (1.0 + mod_ref[0, 1:2, :]) + mod_ref[0, 0:1, :]).astype(o_ref.dtype)


def norm_mod(x, g, mod, seq, *, tm=512, out_dtype=BF16):
    t, d = x.shape
    per_b = seq // tm
    return pl.pallas_call(
        _norm_mod_kernel,
        grid=(t // tm,),
        in_specs=[pl.BlockSpec((tm, d), lambda i: (i, 0)),
                  pl.BlockSpec((1, d), lambda i: (0, 0)),
                  pl.BlockSpec((1, 3, d), lambda i: (i // per_b, 0, 0))],
        out_specs=pl.BlockSpec((tm, d), lambda i: (i, 0)),
        out_shape=jax.ShapeDtypeStruct((t, d), out_dtype),
        compiler_params=_params("arbitrary"),
        name="norm_mod",
    )(x, g, mod)


def _norm_router_kernel(x_ref, g_ref, mod_ref, wr_ref, h_ref, logit_ref):
    y = _rms(x_ref[...]) * g_ref[...]
    h = y * (1.0 + mod_ref[0, 1:2, :]) + mod_ref[0, 0:1, :]
    h_ref[...] = h
    logit_ref[...] = jnp.dot(h, wr_ref[...], preferred_element_type=F32, precision=lax.Precision.HIGHEST)


def norm_router(x, g, mod, w_router_pad, seq, *, tm=512):
    t, d = x.shape
    per_b = seq // tm
    return pl.pallas_call(
        _norm_router_kernel,
        grid=(t // tm,),
        in_specs=[pl.BlockSpec((tm, d), lambda i: (i, 0)),
                  pl.BlockSpec((1, d), lambda i: (0, 0)),
                  pl.BlockSpec((1, 3, d), lambda i: (i // per_b, 0, 0)),
                  pl.BlockSpec((d, LANES), lambda i: (0, 0))],
        out_specs=[pl.BlockSpec((tm, d), lambda i: (i, 0)),
                   pl.BlockSpec((tm, LANES), lambda i: (i, 0))],
        out_shape=[jax.ShapeDtypeStruct((t, d), F32), jax.ShapeDtypeStruct((t, LANES), F32)],
        compiler_params=_params("arbitrary"),
        name="norm_router",
    )(x, g, mod, w_router_pad)


def _final_norm_kernel(x_ref, g_ref, o_ref):
    o_ref[...] = _rms(x_ref[...]) * g_ref[...]


def final_norm(x, g, *, tm=256):
    t, d = x.shape
    return pl.pallas_call(
        _final_norm_kernel,
        grid=(t // tm,),
        in_specs=[pl.BlockSpec((tm, d), lambda i: (i, 0)), pl.BlockSpec((1, d), lambda i: (0, 0))],
        out_specs=pl.BlockSpec((tm, d), lambda i: (i, 0)),
        out_shape=jax.ShapeDtypeStruct((t, d), F32),
        compiler_params=_params("arbitrary"),
    )(x, g)


def _cast_once(w_ref, wbf_ref):
    @pl.when(pl.program_id(1) == 0)
    def _():
        wbf_ref[...] = w_ref[...].astype(BF16)


def _mm_kernel(a_ref, w_ref, o_ref, wbf):
    _cast_once(w_ref, wbf)
    o_ref[...] = _dot(a_ref[...], wbf[...]).astype(o_ref.dtype)


def matmul(a, w, *, tm, tn, out_dtype):
    m, k = a.shape
    n = w.shape[1]
    return pl.pallas_call(
        _mm_kernel,
        grid=(n // tn, m // tm),
        in_specs=[pl.BlockSpec((tm, k), lambda j, i: (i, 0)),
                  pl.BlockSpec((k, tn), lambda j, i: (0, j))],
        out_specs=pl.BlockSpec((tm, tn), lambda j, i: (i, j)),
        out_shape=jax.ShapeDtypeStruct((m, n), out_dtype),
        scratch_shapes=[pltpu.VMEM((k, tn), BF16)],
        compiler_params=_params("arbitrary", "arbitrary"),
        name="mm",
    )(a, w)


def _mm_resid_kernel(a_ref, w_ref, x_ref, mod_ref, o_ref, wbf):
    _cast_once(w_ref, wbf)
    o_ref[...] = x_ref[...] + mod_ref[0, 2:3, :] * _dot(a_ref[...], wbf[...])


def matmul_resid(a, w, x, mod, seq, *, tm, tn):
    m, k = a.shape
    n = w.shape[1]
    per_b = seq // tm
    return pl.pallas_call(
        _mm_resid_kernel,
        grid=(n // tn, m // tm),
        in_specs=[pl.BlockSpec((tm, k), lambda j, i: (i, 0)),
                  pl.BlockSpec((k, tn), lambda j, i: (0, j), pipeline_mode=pl.Buffered(1)),
                  pl.BlockSpec((tm, tn), lambda j, i: (i, j)),
                  pl.BlockSpec((1, 3, tn), lambda j, i: (i // per_b, 0, j))],
        out_specs=pl.BlockSpec((tm, tn), lambda j, i: (i, j)),
        out_shape=jax.ShapeDtypeStruct((m, n), F32),
        scratch_shapes=[pltpu.VMEM((k, tn), BF16)],
        compiler_params=_params("arbitrary", "arbitrary"),
        name="mm_resid",
    )(a, w, x, mod)


def _mm2_resid_kernel(a_ref, b_ref, w_ref, x_ref, mod_ref, o_ref, wbf):
    _cast_once(w_ref, wbf)
    ka = a_ref.shape[1]
    acc = _dot(a_ref[...], wbf[:ka, :]) + _dot(b_ref[...], wbf[ka:, :])
    o_ref[...] = x_ref[...] + mod_ref[0, 2:3, :] * acc


def matmul2_resid(a, b, w, x, mod, seq, *, tm, tn):
    m, ka = a.shape
    kb = b.shape[1]
    n = w.shape[1]
    per_b = seq // tm
    return pl.pallas_call(
        _mm2_resid_kernel,
        grid=(n // tn, m // tm),
        in_specs=[pl.BlockSpec((tm, ka), lambda j, i: (i, 0)),
                  pl.BlockSpec((tm, kb), lambda j, i: (i, 0)),
                  pl.BlockSpec((ka + kb, tn), lambda j, i: (0, j)),
                  pl.BlockSpec((tm, tn), lambda j, i: (i, j)),
                  pl.BlockSpec((1, 3, tn), lambda j, i: (i // per_b, 0, j))],
        out_specs=pl.BlockSpec((tm, tn), lambda j, i: (i, j)),
        out_shape=jax.ShapeDtypeStruct((m, n), F32),
        scratch_shapes=[pltpu.VMEM((ka + kb, tn), BF16)],
        compiler_params=_params("arbitrary", "arbitrary"),
        name="mm2_resid",
    )(a, b, w, x, mod)


def _cast2_once(w1_ref, w2_ref, w1bf, w2bf):
    @pl.when(pl.program_id(1) == 0)
    def _():
        w1bf[...] = w1_ref[...].astype(BF16)
        w2bf[...] = w2_ref[...].astype(BF16)


def _mm_swiglu_kernel(a_ref, wg_ref, wu_ref, o_ref, wgbf, wubf):
    _cast2_once(wg_ref, wu_ref, wgbf, wubf)
    a = a_ref[...]
    g = _dot(a, wgbf[...])
    u = _dot(a, wubf[...])
    o_ref[...] = (g * jax.nn.sigmoid(g) * u).astype(o_ref.dtype)


def matmul_swiglu(a, wg, wu, *, tm, tn):
    m, k = a.shape
    n = wg.shape[1]
    return pl.pallas_call(
        _mm_swiglu_kernel,
        grid=(n // tn, m // tm),
        in_specs=[pl.BlockSpec((tm, k), lambda j, i: (i, 0)),
                  pl.BlockSpec((k, tn), lambda j, i: (0, j)),
                  pl.BlockSpec((k, tn), lambda j, i: (0, j))],
        out_specs=pl.BlockSpec((tm, tn), lambda j, i: (i, j)),
        out_shape=jax.ShapeDtypeStruct((m, n), BF16),
        scratch_shapes=[pltpu.VMEM((k, tn), BF16), pltpu.VMEM((k, tn), BF16)],
        compiler_params=_params("arbitrary", "arbitrary"),
        name="mm_swiglu",
    )(a, wg, wu)


def _mm_glu_resid_kernel(a_ref, wa_ref, wb_ref, x_ref, mod_ref, o_ref, wabf, wbbf):
    _cast2_once(wa_ref, wb_ref, wabf, wbbf)
    a = a_ref[...]
    p = _dot(a, wabf[...])
    q = _dot(a, wbbf[...])
    o_ref[...] = x_ref[...] + mod_ref[0, 2:3, :] * (p * jax.nn.sigmoid(q))


def matmul_glu_resid(a, wa, wb, x, mod, seq, *, tm, tn):
    m, k = a.shape
    n = wa.shape[1]
    per_b = seq // tm
    return pl.pallas_call(
        _mm_glu_resid_kernel,
        grid=(n // tn, m // tm),
        in_specs=[pl.BlockSpec((tm, k), lambda j, i: (i, 0)),
                  pl.BlockSpec((k, tn), lambda j, i: (0, j)),
                  pl.BlockSpec((k, tn), lambda j, i: (0, j)),
                  pl.BlockSpec((tm, tn), lambda j, i: (i, j)),
                  pl.BlockSpec((1, 3, tn), lambda j, i: (i // per_b, 0, j))],
        out_specs=pl.BlockSpec((tm, tn), lambda j, i: (i, j)),
        out_shape=jax.ShapeDtypeStruct((m, n), F32),
        scratch_shapes=[pltpu.VMEM((k, tn), BF16), pltpu.VMEM((k, tn), BF16)],
        compiler_params=_params("arbitrary", "arbitrary"),
        name="mm_glu",
    )(a, wa, wb, x, mod)


def _sb_kernel(q_ref, k_ref, v_ref, o_ref, *, tile, scale, heads):
    qi = pl.program_id(2)
    dh = q_ref.shape[1] // heads
    row = lax.broadcasted_iota(jnp.int32, (tile, tile), 0)
    col = lax.broadcasted_iota(jnp.int32, (tile, tile), 1)
    revcum = jnp.where(row >= col, 1.0, 0.0).astype(BF16)
    key_minus_query = col - row

    def cond(carry):
        step, _, min_later = carry
        return jnp.logical_and(step <= qi, min_later <= SB_SKIP_EXPONENT)

    def body(carry):
        step, state, _ = carry
        kj = qi - step
        start = pl.multiple_of(kj * tile, tile)
        past = key_minus_query < step * tile
        new_state = []
        min_later = None
        for h in range(heads):
            acc, later = state[h]
            lanes = slice(h * dh, (h + 1) * dh)
            kb = k_ref[pl.ds(start, tile), lanes]
            vb = v_ref[pl.ds(start, tile), lanes]
            z = lax.dot_general(q_ref[:, lanes], kb, (((1,), (1,)), ((), ())), preferred_element_type=F32) * scale
            z = jnp.where(past, z, SB_MASKED_SCORE)
            sp = jnp.maximum(z, 0.0) + jnp.log(1.0 + jnp.exp(-jnp.abs(z)))
            sp_hi = sp.astype(BF16)
            sp_lo = (sp - sp_hi.astype(F32)).astype(BF16)
            cum = _dot(sp_hi, revcum) + _dot(sp_lo, revcum)
            w = jnp.exp(z - cum - later)
            acc = acc + _dot(w.astype(BF16), vb)
            later = later + jnp.sum(sp, axis=1, keepdims=True)
            new_state.append((acc, later))
            m = jnp.min(later)
            min_later = m if min_later is None else jnp.minimum(min_later, m)
        return step + 1, tuple(new_state), min_later

    state0 = tuple((jnp.zeros((tile, dh), F32), jnp.zeros((tile, 1), F32)) for _ in range(heads))
    _, state, _ = lax.while_loop(cond, body, (jnp.int32(0), state0, jnp.float32(0.0)))
    for h in range(heads):
        o_ref[:, h * dh:(h + 1) * dh] = state[h][0].astype(o_ref.dtype)


def stick_breaking_attention(qkvz, bsz, seq, n_heads, *, tile=SB_TILE, heads=2):
    t = qkvz.shape[0]
    dh = SB_HEAD_DIM
    nq = seq // tile
    hg = n_heads // heads
    kern = functools.partial(_sb_kernel, tile=tile, scale=1.0 / math.sqrt(dh), heads=heads)
    return pl.pallas_call(
        kern,
        grid=(bsz, hg, nq),
        in_specs=[pl.BlockSpec((tile, heads * dh), lambda b, h, i: (b * nq + i, h)),
                  pl.BlockSpec((seq, heads * dh), lambda b, h, i: (b, hg + h)),
                  pl.BlockSpec((seq, heads * dh), lambda b, h, i: (b, 2 * hg + h))],
        out_specs=pl.BlockSpec((tile, heads * dh), lambda b, h, i: (b * nq + i, h)),
        out_shape=jax.ShapeDtypeStruct((t, n_heads * dh), BF16),
        compiler_params=_params("arbitrary", "arbitrary", "arbitrary"),
        name="sb_attention",
    )(qkvz, qkvz, qkvz)


def _gelu(x):
    return jax.nn.gelu(x, approximate=True)


def _gm_kernel(z1_ref, z2_ref, lng_ref, ws_ref, bs_ref, o_ref, *, chunk):
    u = _gelu(z1_ref[...].astype(F32))
    v = _gelu(z2_ref[...].astype(F32))
    mu = jnp.mean(v, axis=-1, keepdims=True)
    vc = v - mu
    var = jnp.mean(vc * vc, axis=-1, keepdims=True)
    vn = vc * lax.rsqrt(var + EPS) * lng_ref[0]
    row = lax.broadcasted_iota(jnp.int32, (chunk, chunk), 0)
    col = lax.broadcasted_iota(jnp.int32, (chunk, chunk), 1)
    w = jnp.where(row >= col, ws_ref[0], 0.0)
    bias = bs_ref[0]
    for ci in range(u.shape[0] // chunk):
        sl = slice(ci * chunk, (ci + 1) * chunk)
        mixed = jnp.dot(w, vn[sl], preferred_element_type=F32, precision=lax.Precision.HIGHEST) + bias
        o_ref[sl, :] = (u[sl] * mixed).astype(o_ref.dtype)


def spatial_gating(qkvz, ln_g, w_s, b_s, n_groups, *, rows=GM_ROWS):
    t = qkvz.shape[0]
    c = GM_GROUP_DIM
    rows = min(rows, t)
    kern = functools.partial(_gm_kernel, chunk=GM_CHUNK)
    return pl.pallas_call(
        kern,
        grid=(n_groups, t // rows),
        in_specs=[pl.BlockSpec((rows, c), lambda g, i: (i, 3 * n_groups + g)),
                  pl.BlockSpec((rows, c), lambda g, i: (i, 4 * n_groups + g)),
                  pl.BlockSpec((1, 1, c), lambda g, i: (g, 0, 0)),
                  pl.BlockSpec((1, GM_CHUNK, GM_CHUNK), lambda g, i: (g, 0, 0)),
                  pl.BlockSpec((1, GM_CHUNK, 1), lambda g, i: (g, 0, 0))],
        out_specs=pl.BlockSpec((rows, c), lambda g, i: (i, g)),
        out_shape=jax.ShapeDtypeStruct((t, n_groups * c), BF16),
        compiler_params=_params("arbitrary", "arbitrary"),
        name="spatial_gating",
    )(qkvz, qkvz, ln_g[:, None, :], w_s, b_s[:, :, None])


def ssm_operators(lam_re, lam_im, log_dt, b_re, b_im, c_re, c_im):
    g_all, p = lam_re.shape
    cg = SSM_GROUP
    L = SSM_CHUNK
    gpt = LANES // cg
    nt = g_all // gpt
    lr = jnp.minimum(lam_re, -1e-4)
    li = lam_im
    dt = jnp.exp(log_dt)[:, None]
    mag = jnp.exp(lr * dt)
    a_re = mag * jnp.cos(li * dt)
    a_im = mag * jnp.sin(li * dt)
    den = lr * lr + li * li
    nr = a_re - 1.0
    f_re = (nr * lr + a_im * li) / den
    f_im = (a_im * lr - nr * li) / den
    bb_re = f_re[:, :, None] * b_re - f_im[:, :, None] * b_im
    bb_im = f_re[:, :, None] * b_im + f_im[:, :, None] * b_re
    tau = jnp.arange(L + 1, dtype=F32)[:, None, None]
    pmag = jnp.exp(lr * dt * tau)
    pw_re = pmag * jnp.cos(li * dt * tau)
    pw_im = pmag * jnp.sin(li * dt * tau)
    ab_re = pw_re[..., None] * bb_re - pw_im[..., None] * bb_im
    ab_im = pw_re[..., None] * bb_im + pw_im[..., None] * bb_re
    kk = (jnp.einsum('gcp,lgpd->lgcd', c_re, ab_re[:L], precision=lax.Precision.HIGHEST)
          - jnp.einsum('gcp,lgpd->lgcd', c_im, ab_im[:L], precision=lax.Precision.HIGHEST))
    eye = jnp.eye(gpt, dtype=F32)
    k_c = jnp.einsum('ljgcd,gh->jlgdhc', kk.reshape(L, nt, gpt, cg, cg), eye).reshape(nt, L, LANES, LANES)
    def state_in(ab):
        v = ab[:L][::-1].reshape(L, nt, gpt, p, cg).transpose(1, 0, 2, 4, 3).reshape(nt, L, LANES, p)
        return jnp.concatenate([v] * (LANES // p), axis=-1)
    b_c = jnp.stack([state_in(ab_re), state_in(ab_im)], axis=2)
    q_re = pw_re[1:]
    q_im = pw_im[1:]
    co_re = c_re[None] * q_re[:, :, None, :] - c_im[None] * q_im[:, :, None, :]
    co_im = -(c_re[None] * q_im[:, :, None, :] + c_im[None] * q_re[:, :, None, :])
    def state_out(co):
        v = co.reshape(L, nt, gpt, cg, p).transpose(1, 0, 4, 2, 3).reshape(nt, L, p, LANES)
        return jnp.concatenate([v] * (LANES // p), axis=-2)
    c_c = jnp.stack([state_out(co_re), state_out(co_im)], axis=2)
    al_re = pw_re[L].reshape(nt, 1, gpt * p)
    al_im = pw_im[L].reshape(nt, 1, gpt * p)
    return k_c.astype(BF16), b_c.astype(BF16), c_c.astype(BF16), al_re, al_im


def _ssm_expand(kc_ref, bc_ref, cc_ref, m_s, b_s, c_s):
    L = kc_ref.shape[1]
    nstate_half = b_s.shape[1] // 2
    pairs = nstate_half // LANES
    row = lax.broadcasted_iota(jnp.int32, (LANES, LANES), 0)
    lane = lax.broadcasted_iota(jnp.int32, (LANES, LANES), 1)
    per_blk = LANES // SSM_STATE
    zero = jnp.zeros((LANES, LANES), BF16)

    @pl.when(pl.program_id(0) == 0)
    def _():
        m_s[...] = jnp.zeros_like(m_s)

    for s_in in range(L):
        for s_out in range(s_in, L):
            m_s[s_in * LANES:(s_in + 1) * LANES, s_out * LANES:(s_out + 1) * LANES] = kc_ref[0, s_out - s_in]
    for pi in range(pairs):
        in_mask = (row // SSM_GROUP) == (per_blk * pi + lane // SSM_STATE)
        out_mask = (lane // SSM_GROUP) == (per_blk * pi + row // SSM_STATE)
        for s in range(L):
            for ri in range(2):
                col = ri * nstate_half + pi * LANES
                b_s[s * LANES:(s + 1) * LANES, col:col + LANES] = jnp.where(in_mask, bc_ref[0, s, ri], zero)
                c_s[col:col + LANES, s * LANES:(s + 1) * LANES] = jnp.where(out_mask, cc_ref[0, s, ri], zero)


def _ssm_kernel(u_ref, kc_ref, bc_ref, cc_ref, are_ref, aim_ref, d_ref, o_ref,
                m_s, b_s, c_s, xl3, xp3, pw3, ybuf, *, bsz, nseg, seglen, pitch):
    L = kc_ref.shape[1]
    nblk = xl3.shape[0]
    hb = nblk // 2
    nseq = bsz * nseg
    _ssm_expand(kc_ref, bc_ref, cc_ref, m_s, b_s, c_s)

    @pl.when(pl.program_id(0) == 0)
    def _():
        xp3[...] = jnp.zeros_like(xp3)

    zpad = jnp.zeros((pitch - seglen, LANES), F32)
    u_steps = []
    for s in range(L):
        pieces = []
        for m in range(nseq):
            pieces += [u_ref[pl.ds(m * seglen * L + s, seglen, stride=L), :], zpad]
        u_steps.append(jnp.concatenate(pieces, axis=0))
    u2 = jnp.concatenate(u_steps, axis=1).astype(BF16)
    xl = _dot(u2, b_s[...])
    for kb in range(nblk):
        xl3[kb] = xl[:, kb * LANES:(kb + 1) * LANES]
    a_re = [are_ref[0, :, kb * LANES:(kb + 1) * LANES] for kb in range(hb)]
    a_im = [aim_ref[0, :, kb * LANES:(kb + 1) * LANES] for kb in range(hb)]

    seg_pow = []
    for kb in range(hb):
        pw3[kb, 0:1, :] = jnp.ones((1, LANES), F32)
        pw3[hb + kb, 0:1, :] = jnp.zeros((1, LANES), F32)
        qr, qi = a_re[kb], a_im[kb]
        n = 1
        while n < seglen:
            pr, pi = pw3[kb, 0:n, :], pw3[hb + kb, 0:n, :]
            pw3[kb, n:2 * n, :] = pr * qr - pi * qi
            pw3[hb + kb, n:2 * n, :] = pr * qi + pi * qr
            qr, qi = qr * qr - qi * qi, 2.0 * qr * qi
            n *= 2
        seg_pow.append((qr, qi))

    def step(k, xs):
        rows = pl.ds(k, nseq, stride=pitch)
        new_x = []
        for kb in range(hb):
            xr, xi = xs[kb]
            xp3.at[kb][rows, :] = xr
            xp3.at[hb + kb][rows, :] = xi
            lr = xl3.at[kb][rows, :]
            li = xl3.at[hb + kb][rows, :]
            new_x.append((a_re[kb] * xr - a_im[kb] * xi + lr, a_re[kb] * xi + a_im[kb] * xr + li))
        return tuple(new_x)

    x0 = tuple((jnp.zeros((nseq, LANES), F32), jnp.zeros((nseq, LANES), F32)) for _ in range(hb))
    xs = lax.fori_loop(0, seglen, step, x0)

    for kb in range(hb):
        pr = pw3[kb]
        pi = pw3[hb + kb]
        fr, fi = xs[kb]
        qr, qi = seg_pow[kb]
        for b in range(bsz):
            sr = si = None
            for q in range(1, nseg):
                m = b * nseg + q
                er, ei = fr[m - 1:m], fi[m - 1:m]
                if sr is None:
                    sr, si = er, ei
                else:
                    sr, si = er + qr * sr - qi * si, ei + qr * si + qi * sr
                sl = slice(m * pitch, m * pitch + seglen)
                xp3[kb, sl, :] = xp3[kb, sl, :] + (pr * sr - pi * si)
                xp3[hb + kb, sl, :] = xp3[hb + kb, sl, :] + (pr * si + pi * sr)

    xp = jnp.concatenate([xp3[kb] for kb in range(nblk)], axis=1).astype(BF16)
    y = _dot(u2, m_s[...]) + _dot(xp, c_s[...])
    d = d_ref[...]
    for s in range(L):
        ys = _gelu(y[:, s * LANES:(s + 1) * LANES] + d * u_steps[s])
        for m in range(nseq):
            ybuf[pl.ds(m * seglen * L + s, seglen, stride=L), :] = ys[m * pitch:m * pitch + seglen]
    o_ref[...] = ybuf[...].astype(o_ref.dtype)


def ssm_mixer(u, ops, d_skip, bsz):
    k_c, b_c, c_c, al_re, al_im = ops
    t, width = u.shape
    L = k_c.shape[1]
    nrow = t // L
    nt = width // LANES
    nstate = 2 * al_re.shape[2]
    sublanes = 8
    nseg = sublanes // bsz
    seglen = nrow // (bsz * nseg)
    nblk = nstate // LANES
    assert seglen & (seglen - 1) == 0 and nrow == bsz * nseg * seglen
    pitch = seglen + sublanes
    prow = bsz * nseg * pitch
    kern = functools.partial(_ssm_kernel, bsz=bsz, nseg=nseg, seglen=seglen, pitch=pitch)
    return pl.pallas_call(
        kern,
        grid=(nt,),
        in_specs=[pl.BlockSpec((t, LANES), lambda j: (0, j)),
                  pl.BlockSpec((1, L, LANES, LANES), lambda j: (j, 0, 0, 0)),
                  pl.BlockSpec((1, L, 2, LANES, LANES), lambda j: (j, 0, 0, 0, 0)),
                  pl.BlockSpec((1, L, 2, LANES, LANES), lambda j: (j, 0, 0, 0, 0)),
                  pl.BlockSpec((1, 1, nstate // 2), lambda j: (j, 0, 0)),
                  pl.BlockSpec((1, 1, nstate // 2), lambda j: (j, 0, 0)),
                  pl.BlockSpec((1, LANES), lambda j: (0, j))],
        out_specs=pl.BlockSpec((t, LANES), lambda j: (0, j)),
        out_shape=jax.ShapeDtypeStruct((t, width), BF16),
        scratch_shapes=[pltpu.VMEM((L * LANES, L * LANES), BF16), pltpu.VMEM((L * LANES, nstate), BF16),
                        pltpu.VMEM((nstate, L * LANES), BF16),
                        pltpu.VMEM((nblk, prow, LANES), F32), pltpu.VMEM((nblk, prow, LANES), F32),
                        pltpu.VMEM((nblk, seglen, LANES), F32), pltpu.VMEM((t, LANES), F32)],
        compiler_params=_params("arbitrary"),
        name="ssm_mixer",
    )(u, k_c, b_c, c_c, al_re, al_im, d_skip)


def _route_kernel(logit_ref, idx_ref, gate_ref, *, n_experts):
    lg = logit_ref[...]
    lane = lax.broadcasted_iota(jnp.int32, lg.shape, 1)
    neg = jnp.float32(-jnp.inf)
    lg = jnp.where(lane < n_experts, lg, neg)
    m1 = jnp.max(lg, axis=-1, keepdims=True)
    i1 = jnp.min(jnp.where(lg == m1, lane, LANES), axis=-1, keepdims=True)
    lg2 = jnp.where(lane == i1, neg, lg)
    m2 = jnp.max(lg2, axis=-1, keepdims=True)
    i2 = jnp.min(jnp.where(lg2 == m2, lane, LANES), axis=-1, keepdims=True)
    e2 = jnp.exp(m2 - m1)
    p1 = 1.0 / (1.0 + e2)
    p2 = e2 / (1.0 + e2)
    idx_ref[...] = jnp.where(lane == 0, i1, jnp.where(lane == 1, i2, 0))
    gate_ref[...] = jnp.where(lane == 0, p1, jnp.where(lane == 1, p2, 0.0))


def route_top2(logits, *, tm=512):
    t = logits.shape[0]
    kern = functools.partial(_route_kernel, n_experts=N_EXPERTS)
    return pl.pallas_call(
        kern,
        grid=(t // tm,),
        in_specs=[pl.BlockSpec((tm, LANES), lambda i: (i, 0))],
        out_specs=[pl.BlockSpec((tm, LANES), lambda i: (i, 0)), pl.BlockSpec((tm, LANES), lambda i: (i, 0))],
        out_shape=[jax.ShapeDtypeStruct((t, LANES), jnp.int32), jax.ShapeDtypeStruct((t, LANES), F32)],
        compiler_params=_params("arbitrary"),
    )(logits)


def _dispatch_kernel(tok_ref, h_hbm, o_ref, buf, sems, *, tile):
    i = pl.program_id(0)

    def row_copy(step, r):
        slot = step % 2
        return pltpu.make_async_copy(h_hbm.at[pl.ds(tok_ref[step * tile + r], 1)],
                                     buf.at[slot, pl.ds(r, 1)], sems.at[slot])

    def issue_step(step):
        def issue(k, c):
            row_copy(step, 2 * k).start(priority=0)
            row_copy(step, 2 * k + 1).start(priority=1)
            return c
        lax.fori_loop(0, tile // 2, issue, 0)

    @pl.when(i == 0)
    def _():
        issue_step(i)

    @pl.when(i + 1 < pl.num_programs(0))
    def _():
        issue_step(i + 1)

    def drain(r, c):
        row_copy(i, r).wait()
        return c

    lax.fori_loop(0, tile, drain, 0)
    o_ref[...] = buf[i % 2].astype(o_ref.dtype)


def moe_dispatch(h, tok_of_slot, n_slots, *, tile=MOE_TILE):
    d = h.shape[1]
    kern = functools.partial(_dispatch_kernel, tile=tile)
    return pl.pallas_call(
        kern,
        grid_spec=pltpu.PrefetchScalarGridSpec(
            num_scalar_prefetch=1,
            grid=(n_slots // tile,),
            in_specs=[pl.BlockSpec(memory_space=pl.ANY)],
            out_specs=pl.BlockSpec((tile, d), lambda i, tok: (i, 0)),
            scratch_shapes=[pltpu.VMEM((2, tile, d), F32), pltpu.SemaphoreType.DMA((2,))]),
        out_shape=jax.ShapeDtypeStruct((n_slots, d), BF16),
        compiler_params=_params("arbitrary"),
        name="moe_dispatch",
    )(tok_of_slot, h)


def _stream_expert_weights(te_ref, nre_ref, w_hbms, stages, wbfs, sems, tn):
    j = pl.program_id(0)
    i = pl.program_id(1)
    e = te_ref[i]
    run_start = jnp.logical_or(i == 0, e != te_ref[jnp.maximum(i - 1, 0)])

    def copies(ex, jx):
        col = pl.multiple_of(jx * tn, tn)
        return [pltpu.make_async_copy(w.at[ex, :, pl.ds(col, tn)], st, sems.at[k])
                for k, (w, st) in enumerate(zip(w_hbms, stages))]

    @pl.when(run_start)
    def _():
        @pl.when(jnp.logical_and(j == 0, i == 0))
        def _():
            for cp in copies(e, j):
                cp.start()

        for cp in copies(e, j):
            cp.wait()
        for st, wb in zip(stages, wbfs):
            wb[...] = st[...].astype(BF16)
        nxt = nre_ref[i]
        last = nxt < 0
        e_next = jnp.where(last, te_ref[0], nxt)
        j_next = jnp.where(last, j + 1, j)

        @pl.when(jnp.logical_not(jnp.logical_and(last, j == pl.num_programs(0) - 1)))
        def _():
            for cp in copies(e_next, j_next):
                cp.start()


def _moe_up_kernel(te_ref, na_ref, nre_ref, a_ref, wg_hbm, wu_hbm, o_ref, stg, stu, wgbf, wubf, sems, *, tn):
    _stream_expert_weights(te_ref, nre_ref, (wg_hbm, wu_hbm), (stg, stu), (wgbf, wubf), sems, tn)

    @pl.when(pl.program_id(1) < na_ref[0])
    def _():
        a = a_ref[...]
        g = _dot(a, wgbf[...])
        u = _dot(a, wubf[...])
        o_ref[...] = (g * jax.nn.sigmoid(g) * u).astype(o_ref.dtype)

    @pl.when(pl.program_id(1) >= na_ref[0])
    def _():
        o_ref[...] = jnp.zeros_like(o_ref)


def moe_up(hg, plan, wg, wu, *, tile=MOE_TILE, tn=512):
    tile_expert, n_active, next_run_expert = plan
    r, k = hg.shape
    n = wg.shape[2]
    tn = min(tn, n)
    kern = functools.partial(_moe_up_kernel, tn=tn)
    return pl.pallas_call(
        kern,
        grid_spec=pltpu.PrefetchScalarGridSpec(
            num_scalar_prefetch=3,
            grid=(n // tn, r // tile),
            in_specs=[pl.BlockSpec((tile, k), lambda j, i, te, na, nre: (i, 0)),
                      pl.BlockSpec(memory_space=pl.ANY),
                      pl.BlockSpec(memory_space=pl.ANY)],
            out_specs=pl.BlockSpec((tile, tn), lambda j, i, te, na, nre: (i, j)),
            scratch_shapes=[pltpu.VMEM((k, tn), F32), pltpu.VMEM((k, tn), F32),
                            pltpu.VMEM((k, tn), BF16), pltpu.VMEM((k, tn), BF16),
                            pltpu.SemaphoreType.DMA((2,))]),
        out_shape=jax.ShapeDtypeStruct((r, n), BF16),
        compiler_params=_params("arbitrary", "arbitrary"),
        name="moe_up",
    )(tile_expert, n_active, next_run_expert, hg, wg, wu)


def _moe_down_kernel(te_ref, na_ref, nre_ref, a_ref, w_hbm, o_ref, stage, wbf, sems, *, tn):
    _stream_expert_weights(te_ref, nre_ref, (w_hbm,), (stage,), (wbf,), sems, tn)

    @pl.when(pl.program_id(1) < na_ref[0])
    def _():
        o_ref[...] = _dot(a_ref[...], wbf[...])

    @pl.when(pl.program_id(1) >= na_ref[0])
    def _():
        o_ref[...] = jnp.zeros_like(o_ref)


def moe_down(act, plan, wd, *, tile=MOE_TILE, tn=1024):
    tile_expert, n_active, next_run_expert = plan
    r, k = act.shape
    n = wd.shape[2]
    tn = min(tn, n)
    kern = functools.partial(_moe_down_kernel, tn=tn)
    return pl.pallas_call(
        kern,
        grid_spec=pltpu.PrefetchScalarGridSpec(
            num_scalar_prefetch=3,
            grid=(n // tn, r // tile),
            in_specs=[pl.BlockSpec((tile, k), lambda j, i, te, na, nre: (i, 0)),
                      pl.BlockSpec(memory_space=pl.ANY)],
            out_specs=pl.BlockSpec((tile, tn), lambda j, i, te, na, nre: (i, j)),
            scratch_shapes=[pltpu.VMEM((k, tn), F32), pltpu.VMEM((k, tn), BF16),
                            pltpu.SemaphoreType.DMA((1,))]),
        out_shape=jax.ShapeDtypeStruct((r, n), F32),
        compiler_params=_params("arbitrary", "arbitrary"),
        name="moe_down",
    )(tile_expert, n_active, next_run_expert, act, wd)


def _combine_kernel(pos_ref, og_hbm, x_ref, gate_ref, mod_ref, fg_ref, o_ref, buf, sems, *, tm):
    i = pl.program_id(0)

    def copies(step, r):
        slot = step % 2
        t = step * tm + r
        return tuple(pltpu.make_async_copy(og_hbm.at[pl.ds(pos_ref[2 * t + c], 1)],
                                           buf.at[slot, c, pl.ds(r, 1)], sems.at[slot, c]) for c in range(2))

    def issue_step(step):
        def issue(r, carry):
            for c, cp in enumerate(copies(step, r)):
                cp.start(priority=c)
            return carry
        lax.fori_loop(0, tm, issue, 0)

    @pl.when(i == 0)
    def _():
        issue_step(i)

    @pl.when(i + 1 < pl.num_programs(0))
    def _():
        issue_step(i + 1)

    def drain(r, carry):
        for cp in copies(i, r):
            cp.wait()
        return carry

    lax.fori_loop(0, tm, drain, 0)
    g = gate_ref[...]
    slot = i % 2
    moe = g[:, 0:1] * buf[slot, 0] + g[:, 1:2] * buf[slot, 1]
    xn = x_ref[...] + mod_ref[0, 2:3, :] * moe
    o_ref[...] = _rms(xn) * fg_ref[...]


def moe_combine_final(og, pos_flat, x, gates, mod, final_g, seq, *, tm=128):
    t, d = x.shape
    per_b = seq // tm
    kern = functools.partial(_combine_kernel, tm=tm)
    return pl.pallas_call(
        kern,
        grid_spec=pltpu.PrefetchScalarGridSpec(
            num_scalar_prefetch=1,
            grid=(t // tm,),
            in_specs=[pl.BlockSpec(memory_space=pl.ANY),
                      pl.BlockSpec((tm, d), lambda i, pos: (i, 0)),
                      pl.BlockSpec((tm, LANES), lambda i, pos: (i, 0)),
                      pl.BlockSpec((1, 3, d), lambda i, pos: (i // per_b, 0, 0)),
                      pl.BlockSpec((1, d), lambda i, pos: (0, 0))],
            out_specs=pl.BlockSpec((tm, d), lambda i, pos: (i, 0)),
            scratch_shapes=[pltpu.VMEM((2, 2, tm, d), F32), pltpu.SemaphoreType.DMA((2, 2))]),
        out_shape=jax.ShapeDtypeStruct((t, d), F32),
        compiler_params=_params("arbitrary"),
        name="moe_combine",
    )(pos_flat, og, x, gates, mod, final_g)


def moe_plan(idx, n_tokens, *, tile=MOE_TILE):
    e1 = idx[:, 0]
    e2 = idx[:, 1]
    flat_e = jnp.stack([e1, e2], axis=1).reshape(-1)
    onehot = (flat_e[:, None] == jnp.arange(N_EXPERTS)[None, :]).astype(jnp.int32)
    rank = jnp.cumsum(onehot, axis=0) - onehot
    counts = jnp.sum(onehot, axis=0)
    padded = ((counts + tile - 1) // tile) * tile
    ends = jnp.cumsum(padded)
    starts = ends - padded
    pos = jnp.sum(onehot * (rank + starts[None, :]), axis=1)
    n_slots = 2 * n_tokens + N_EXPERTS * tile
    n_tiles = n_slots // tile
    tok = jnp.repeat(jnp.arange(n_tokens, dtype=jnp.int32), 2)
    tok_of_slot = jnp.zeros((n_slots,), jnp.int32).at[pos].set(tok)
    n_active = (ends[-1] // tile).astype(jnp.int32)
    tile_start = jnp.arange(n_tiles, dtype=jnp.int32) * tile
    tile_expert = jnp.sum((tile_start[:, None] >= ends[None, :]).astype(jnp.int32), axis=1)
    last_expert = tile_expert[jnp.maximum(n_active - 1, 0)]
    tile_expert = jnp.where(jnp.arange(n_tiles) < n_active, tile_expert, last_expert).astype(jnp.int32)
    ids = jnp.arange(N_EXPERTS)
    later_present = jnp.logical_and(ids[None, :] > ids[:, None], (counts > 0)[None, :])
    next_present = jnp.min(jnp.where(later_present, ids[None, :], N_EXPERTS), axis=1)
    next_present = jnp.where(next_present == N_EXPERTS, -1, next_present).astype(jnp.int32)
    next_run_expert = next_present[tile_expert]
    plan = (tile_expert, n_active.reshape(1), next_run_expert)
    return pos.astype(jnp.int32), tok_of_slot, plan, n_slots


def kernel(x, c, mix0_norm_g, mix0_ada_w, mix0_ada_b, mix0_w_in, gm_ln_g, gm_w_s, gm_b_s, mix0_w_out, ffn0_norm_g, ffn0_ada_w, ffn0_ada_b, ffn0_w_gate, ffn0_w_up, ffn0_w_down, mix1_norm_g, mix1_ada_w, mix1_ada_b, ssm_w_in, ssm_lam_re, ssm_lam_im, ssm_log_dt, ssm_b_re, ssm_b_im, ssm_c_re, ssm_c_im, ssm_d, glu_w_a, glu_w_b, moe_norm_g, moe_ada_w, moe_ada_b, moe_w_router, moe_w_gate, moe_w_up, moe_w_down, final_norm_g):
    bsz, seq, d = x.shape
    t = bsz * seq
    n_heads = mix0_w_in.shape[2] // (5 * SB_HEAD_DIM)
    n_groups = gm_w_s.shape[1]
    xf = x.reshape(t, d)
    c_pad = jnp.zeros((8, d), F32).at[:bsz].set(c)
    tm_big = min(1024, seq)

    def ada(w, b):
        m = ada_params(c_pad, w[0], b[0][None, :])
        return m[:bsz].reshape(bsz, 3, d)

    mod = ada(mix0_ada_w, mix0_ada_b)
    h = norm_mod(xf, mix0_norm_g, mod, seq)
    qkvz = matmul(h, mix0_w_in[0], tm=tm_big,tn=512, out_dtype=BF16)
    a_out = stick_breaking_attention(qkvz, bsz, seq, n_heads)
    b_out = spatial_gating(qkvz, gm_ln_g[0], gm_w_s[0], gm_b_s[0], n_groups)
    xf = matmul2_resid(a_out, b_out, mix0_w_out[0], xf, mod, seq, tm=tm_big,tn=512)
    mod = ada(ffn0_ada_w, ffn0_ada_b)
    h = norm_mod(xf, ffn0_norm_g, mod, seq)
    act = matmul_swiglu(h, ffn0_w_gate[0], ffn0_w_up[0], tm=tm_big,tn=256)
    xf = matmul_resid(act, ffn0_w_down[0], xf, mod, seq, tm=256, tn=512)
    mod = ada(mix1_ada_w, mix1_ada_b)
    h = norm_mod(xf, mix1_norm_g, mod, seq)
    u = matmul(h, ssm_w_in[0], tm=tm_big,tn=512, out_dtype=F32)
    ops = ssm_operators(ssm_lam_re[0], ssm_lam_im[0], ssm_log_dt[0], ssm_b_re[0], ssm_b_im[0],
                        ssm_c_re[0], ssm_c_im[0])
    y = ssm_mixer(u, ops, ssm_d, bsz)
    xf = matmul_glu_resid(y, glu_w_a[0], glu_w_b[0], xf, mod, seq, tm=tm_big,tn=256)
    mod = ada(moe_ada_w, moe_ada_b)
    w_router_pad = jnp.zeros((d, LANES), F32).at[:, :N_EXPERTS].set(moe_w_router[0])
    h32, logits = norm_router(xf, moe_norm_g, mod, w_router_pad, seq)
    idx, gates = route_top2(logits)
    pos, tok_of_slot, plan, n_slots = moe_plan(idx, t)
    hg = moe_dispatch(h32, tok_of_slot, n_slots)
    act = moe_up(hg, plan, moe_w_gate[0], moe_w_up[0])
    og = moe_down(act, plan, moe_w_down[0])
    out = moe_combine_final(og, pos, xf, gates, mod, final_norm_g[None, :], seq)
    return out.reshape(bsz, seq, d)
```

```python
import functools
import math

import jax
import jax.numpy as jnp
from jax import lax
from jax.experimental import pallas as pl
from jax.experimental.pallas import tpu as pltpu

EPS = 1e-6
SB_HEAD_DIM = 128
SB_TILE = 256
GM_GROUP_DIM = 128
GM_CHUNK = 128
GM_ROWS = 2048
SSM_GROUP = 16
SSM_STATE = 64
SSM_CHUNK = 8
LANES = 128
N_EXPERTS = 8
MOE_TILE = 256
SB_SKIP_EXPONENT = 104.0
SB_MASKED_SCORE = -1e30
VMEM_LIMIT_BYTES = 56 * 1024 * 1024

BF16 = jnp.bfloat16
F32 = jnp.float32


def _params(*sem):
    return pltpu.CompilerParams(dimension_semantics=sem, vmem_limit_bytes=VMEM_LIMIT_BYTES)


def _dot(a, b):
    return jnp.dot(a, b, preferred_element_type=F32)


def _ada_kernel(c_ref, w_ref, b_ref, o_ref):
    c = c_ref[...]
    s = (c * jax.nn.sigmoid(c))
    s_hi = s.astype(BF16)
    s_lo = (s - s_hi.astype(F32)).astype(BF16)
    w = w_ref[...].astype(BF16)
    o_ref[...] = _dot(s_hi, w) + _dot(s_lo, w) + b_ref[...]


def ada_params(c_pad, w, b, *, tn=512):
    rows, d = c_pad.shape
    n = w.shape[1]
    return pl.pallas_call(
        _ada_kernel,
        grid=(n // tn,),
        in_specs=[pl.BlockSpec((rows, d), lambda j: (0, 0)),
                  pl.BlockSpec((d, tn), lambda j: (0, j)),
                  pl.BlockSpec((1, tn), lambda j: (0, j))],
        out_specs=pl.BlockSpec((rows, tn), lambda j: (0, j)),
        out_shape=jax.ShapeDtypeStruct((rows, n), F32),
        compiler_params=_params("arbitrary"),
        name="ada_params",
    )(c_pad, w, b)


def _rms(x):
    return x * lax.rsqrt(jnp.mean(x * x, axis=-1, keepdims=True) + EPS)


def _norm_mod_kernel(x_ref, g_ref, mod_ref, o_ref):
    y = _rms(x_ref[...]) * g_ref[...]
    o_ref[...] = (y * (1.0 + mod_ref[0, 1:2, :]) + mod_ref[0, 0:1, :]).astype(o_ref.dtype)


def norm_mod(x, g, mod, seq, *, tm=512, out_dtype=BF16):
    t, d = x.shape
    per_b = seq // tm
    return pl.pallas_call(
        _norm_mod_kernel,
        grid=(t // tm,),
        in_specs=[pl.BlockSpec((tm, d), lambda i: (i, 0)),
                  pl.BlockSpec((1, d), lambda i: (0, 0)),
                  pl.BlockSpec((1, 3, d), lambda i: (i // per_b, 0, 0))],
        out_specs=pl.BlockSpec((tm, d), lambda i: (i, 0)),
        out_shape=jax.ShapeDtypeStruct((t, d), out_dtype),
        compiler_params=_params("arbitrary"),
        name="norm_mod",
    )(x, g, mod)


def _norm_router_kernel(x_ref, g_ref, mod_ref, wr_ref, h_ref, logit_ref):
    y = _rms(x_ref[...]) * g_ref[...]
    h = y * (1.0 + mod_ref[0, 1:2, :]) + mod_ref[0, 0:1, :]
    h_ref[...] = h
    logit_ref[...] = jnp.dot(h, wr_ref[...], preferred_element_type=F32, precision=lax.Precision.HIGHEST)


def norm_router(x, g, mod, w_router_pad, seq, *, tm=512):
    t, d = x.shape
    per_b = seq // tm
    return pl.pallas_call(
        _norm_router_kernel,
        grid=(t // tm,),
        in_specs=[pl.BlockSpec((tm, d), lambda i: (i, 0)),
                  pl.BlockSpec((1, d), lambda i: (0, 0)),
                  pl.BlockSpec((1, 3, d), lambda i: (i // per_b, 0, 0)),
                  pl.BlockSpec((d, LANES), lambda i: (0, 0))],
        out_specs=[pl.BlockSpec((tm, d), lambda i: (i, 0)),
                   pl.BlockSpec((tm, LANES), lambda i: (i, 0))],
        out_shape=[jax.ShapeDtypeStruct((t, d), F32), jax.ShapeDtypeStruct((t, LANES), F32)],
        compiler_params=_params("arbitrary"),
        name="norm_router",
    )(x, g, mod, w_router_pad)


def _cast_once(w_ref, wbf_ref):
    @pl.when(pl.program_id(1) == 0)
    def _():
        wbf_ref[...] = w_ref[...].astype(BF16)


def _mm_kernel(a_ref, w_ref, o_ref, wbf):
    _cast_once(w_ref, wbf)
    o_ref[...] = _dot(a_ref[...], wbf[...]).astype(o_ref.dtype)


def matmul(a, w, *, tm, tn, out_dtype):
    m, k = a.shape
    n = w.shape[1]
    return pl.pallas_call(
        _mm_kernel,
        grid=(n // tn, m // tm),
        in_specs=[pl.BlockSpec((tm, k), lambda j, i: (i, 0)),
                  pl.BlockSpec((k, tn), lambda j, i: (0, j))],
        out_specs=pl.BlockSpec((tm, tn), lambda j, i: (i, j)),
        out_shape=jax.ShapeDtypeStruct((m, n), out_dtype),
        scratch_shapes=[pltpu.VMEM((k, tn), BF16)],
        compiler_params=_params("arbitrary", "arbitrary"),
        name="mm",
    )(a, w)


def _mm_resid_kernel(a_ref, w_ref, x_ref, mod_ref, o_ref, wbf):
    _cast_once(w_ref, wbf)
    o_ref[...] = x_ref[...] + mod_ref[0, 2:3, :] * _dot(a_ref[...], wbf[...])


def matmul_resid(a, w, x, mod, seq, *, tm, tn):
    m, k = a.shape
    n = w.shape[1]
    per_b = seq // tm
    return pl.pallas_call(
        _mm_resid_kernel,
        grid=(n // tn, m // tm),
        in_specs=[pl.BlockSpec((tm, k), lambda j, i: (i, 0)),
                  pl.BlockSpec((k, tn), lambda j, i: (0, j), pipeline_mode=pl.Buffered(1)),
                  pl.BlockSpec((tm, tn), lambda j, i: (i, j)),
                  pl.BlockSpec((1, 3, tn), lambda j, i: (i // per_b, 0, j))],
        out_specs=pl.BlockSpec((tm, tn), lambda j, i: (i, j)),
        out_shape=jax.ShapeDtypeStruct((m, n), F32),
        scratch_shapes=[pltpu.VMEM((k, tn), BF16)],
        compiler_params=_params("arbitrary", "arbitrary"),
        name="mm_resid",
    )(a, w, x, mod)


def _mm2_resid_kernel(a_ref, b_ref, w_ref, x_ref, mod_ref, o_ref, wbf):
    _cast_once(w_ref, wbf)
    ka = a_ref.shape[1]
    acc = _dot(a_ref[...], wbf[:ka, :]) + _dot(b_ref[...], wbf[ka:, :])
    o_ref[...] = x_ref[...] + mod_ref[0, 2:3, :] * acc


def matmul2_resid(a, b, w, x, mod, seq, *, tm, tn):
    m, ka = a.shape
    kb = b.shape[1]
    n = w.shape[1]
    per_b = seq // tm
    return pl.pallas_call(
        _mm2_resid_kernel,
        grid=(n // tn, m // tm),
        in_specs=[pl.BlockSpec((tm, ka), lambda j, i: (i, 0)),
                  pl.BlockSpec((tm, kb), lambda j, i: (i, 0)),
                  pl.BlockSpec((ka + kb, tn), lambda j, i: (0, j)),
                  pl.BlockSpec((tm, tn), lambda j, i: (i, j)),
                  pl.BlockSpec((1, 3, tn), lambda j, i: (i // per_b, 0, j))],
        out_specs=pl.BlockSpec((tm, tn), lambda j, i: (i, j)),
        out_shape=jax.ShapeDtypeStruct((m, n), F32),
        scratch_shapes=[pltpu.VMEM((ka + kb, tn), BF16)],
        compiler_params=_params("arbitrary", "arbitrary"),
        name="mm2_resid",
    )(a, b, w, x, mod)


def _cast2_once(w1_ref, w2_ref, w1bf, w2bf):
    @pl.when(pl.program_id(1) == 0)
    def _():
        w1bf[...] = w1_ref[...].astype(BF16)
        w2bf[...] = w2_ref[...].astype(BF16)


def _mm_swiglu_kernel(a_ref, wg_ref, wu_ref, o_ref, wgbf, wubf):
    _cast2_once(wg_ref, wu_ref, wgbf, wubf)
    a = a_ref[...]
    g = _dot(a, wgbf[...])
    u = _dot(a, wubf[...])
    o_ref[...] = (g * jax.nn.sigmoid(g) * u).astype(o_ref.dtype)


def matmul_swiglu(a, wg, wu, *, tm, tn):
    m, k = a.shape
    n = wg.shape[1]
    return pl.pallas_call(
        _mm_swiglu_kernel,
        grid=(n // tn, m // tm),
        in_specs=[pl.BlockSpec((tm, k), lambda j, i: (i, 0)),
                  pl.BlockSpec((k, tn), lambda j, i: (0, j)),
                  pl.BlockSpec((k, tn), lambda j, i: (0, j))],
        out_specs=pl.BlockSpec((tm, tn), lambda j, i: (i, j)),
        out_shape=jax.ShapeDtypeStruct((m, n), BF16),
        scratch_shapes=[pltpu.VMEM((k, tn), BF16), pltpu.VMEM((k, tn), BF16)],
        compiler_params=_params("arbitrary", "arbitrary"),
        name="mm_swiglu",
    )(a, wg, wu)


def _mm_glu_resid_kernel(a_ref, wa_ref, wb_ref, x_ref, mod_ref, o_ref, wabf, wbbf):
    _cast2_once(wa_ref, wb_ref, wabf, wbbf)
    a = a_ref[...]
    p = _dot(a, wabf[...])
    q = _dot(a, wbbf[...])
    o_ref[...] = x_ref[...] + mod_ref[0, 2:3, :] * (p * jax.nn.sigmoid(q))


def matmul_glu_resid(a, wa, wb, x, mod, seq, *, tm, tn):
    m, k = a.shape
    n = wa.shape[1]
    per_b = seq // tm
    return pl.pallas_call(
        _mm_glu_resid_kernel,
        grid=(n // tn, m // tm),
        in_specs=[pl.BlockSpec((tm, k), lambda j, i: (i, 0)),
                  pl.BlockSpec((k, tn), lambda j, i: (0, j)),
                  pl.BlockSpec((k, tn), lambda j, i: (0, j)),
                  pl.BlockSpec((tm, tn), lambda j, i: (i, j)),
                  pl.BlockSpec((1, 3, tn), lambda j, i: (i // per_b, 0, j))],
        out_specs=pl.BlockSpec((tm, tn), lambda j, i: (i, j)),
        out_shape=jax.ShapeDtypeStruct((m, n), F32),
        scratch_shapes=[pltpu.VMEM((k, tn), BF16), pltpu.VMEM((k, tn), BF16)],
        compiler_params=_params("arbitrary", "arbitrary"),
        name="mm_glu",
    )(a, wa, wb, x, mod)


def _sb_kernel(q_ref, k_ref, v_ref, o_ref, *, tile, scale, heads):
    qi = pl.program_id(2)
    dh = q_ref.shape[1] // heads
    row = lax.broadcasted_iota(jnp.int32, (tile, tile), 0)
    col = lax.broadcasted_iota(jnp.int32, (tile, tile), 1)
    revcum = jnp.where(row >= col, 1.0, 0.0).astype(BF16)
    key_minus_query = col - row

    def cond(carry):
        step, _, min_later = carry
        return jnp.logical_and(step <= qi, min_later <= SB_SKIP_EXPONENT)

    def body(carry):
        step, state, _ = carry
        kj = qi - step
        start = pl.multiple_of(kj * tile, tile)
        past = key_minus_query < step * tile
        hs = range(heads)
        lanes = [slice(h * dh, (h + 1) * dh) for h in hs]
        zs = [lax.dot_general(q_ref[:, lanes[h]], k_ref[pl.ds(start, tile), lanes[h]], (((1,), (1,)), ((), ())),
                              preferred_element_type=F32) * scale for h in hs]
        zs = [jnp.where(past, z, SB_MASKED_SCORE) for z in zs]
        sps = [jnp.maximum(z, 0.0) + jnp.log(1.0 + jnp.exp(-jnp.abs(z))) for z in zs]
        his = [sp.astype(BF16) for sp in sps]
        los = [(sp - hi.astype(F32)).astype(BF16) for sp, hi in zip(sps, his)]
        cums = [_dot(hi, revcum) + _dot(lo, revcum) for hi, lo in zip(his, los)]
        ws = [jnp.exp(z - cum - state[h][1]).astype(BF16) for h, (z, cum) in enumerate(zip(zs, cums))]
        accs = [state[h][0] + _dot(ws[h], v_ref[pl.ds(start, tile), lanes[h]]) for h in hs]
        laters = [state[h][1] + jnp.sum(sps[h], axis=1, keepdims=True) for h in hs]
        min_later = functools.reduce(jnp.minimum, [jnp.min(later) for later in laters])
        return step + 1, tuple(zip(accs, laters)), min_later

    state0 = tuple((jnp.zeros((tile, dh), F32), jnp.zeros((tile, 1), F32)) for _ in range(heads))
    _, state, _ = lax.while_loop(cond, body, (jnp.int32(0), state0, jnp.float32(0.0)))
    for h in range(heads):
        o_ref[:, h * dh:(h + 1) * dh] = state[h][0].astype(o_ref.dtype)


def stick_breaking_attention(qkvz, bsz, seq, n_heads, *, tile=SB_TILE, heads=4):
    t = qkvz.shape[0]
    dh = SB_HEAD_DIM
    nq = seq // tile
    hg = n_heads // heads
    kern = functools.partial(_sb_kernel, tile=tile, scale=1.0 / math.sqrt(dh), heads=heads)
    return pl.pallas_call(
        kern,
        grid=(bsz, hg, nq),
        in_specs=[pl.BlockSpec((tile, heads * dh), lambda b, h, i: (b * nq + i, h)),
                  pl.BlockSpec((seq, heads * dh), lambda b, h, i: (b, hg + h)),
                  pl.BlockSpec((seq, heads * dh), lambda b, h, i: (b, 2 * hg + h))],
        out_specs=pl.BlockSpec((tile, heads * dh), lambda b, h, i: (b * nq + i, h)),
        out_shape=jax.ShapeDtypeStruct((t, n_heads * dh), BF16),
        compiler_params=_params("arbitrary", "arbitrary", "arbitrary"),
        name="sb_attention",
    )(qkvz, qkvz, qkvz)


def _gelu(x):
    return jax.nn.gelu(x, approximate=True)


def _gm_kernel(z1_ref, z2_ref, lng_ref, ws_ref, bs_ref, o_ref, *, chunk):
    u = _gelu(z1_ref[...].astype(F32))
    v = _gelu(z2_ref[...].astype(F32))
    mu = jnp.mean(v, axis=-1, keepdims=True)
    vc = v - mu
    var = jnp.mean(vc * vc, axis=-1, keepdims=True)
    vn = vc * lax.rsqrt(var + EPS) * lng_ref[0]
    row = lax.broadcasted_iota(jnp.int32, (chunk, chunk), 0)
    col = lax.broadcasted_iota(jnp.int32, (chunk, chunk), 1)
    w = jnp.where(row >= col, ws_ref[0], 0.0)
    bias = bs_ref[0]
    for ci in range(u.shape[0] // chunk):
        sl = slice(ci * chunk, (ci + 1) * chunk)
        mixed = jnp.dot(w, vn[sl], preferred_element_type=F32, precision=lax.Precision.HIGHEST) + bias
        o_ref[sl, :] = (u[sl] * mixed).astype(o_ref.dtype)


def spatial_gating(qkvz, ln_g, w_s, b_s, n_groups, *, rows=GM_ROWS):
    t = qkvz.shape[0]
    c = GM_GROUP_DIM
    rows = min(rows, t)
    kern = functools.partial(_gm_kernel, chunk=GM_CHUNK)
    return pl.pallas_call(
        kern,
        grid=(n_groups, t // rows),
        in_specs=[pl.BlockSpec((rows, c), lambda g, i: (i, 3 * n_groups + g)),
                  pl.BlockSpec((rows, c), lambda g, i: (i, 4 * n_groups + g)),
                  pl.BlockSpec((1, 1, c), lambda g, i: (g, 0, 0)),
                  pl.BlockSpec((1, GM_CHUNK, GM_CHUNK), lambda g, i: (g, 0, 0)),
                  pl.BlockSpec((1, GM_CHUNK, 1), lambda g, i: (g, 0, 0))],
        out_specs=pl.BlockSpec((rows, c), lambda g, i: (i, g)),
        out_shape=jax.ShapeDtypeStruct((t, n_groups * c), BF16),
        compiler_params=_params("arbitrary", "arbitrary"),
        name="spatial_gating",
    )(qkvz, qkvz, ln_g[:, None, :], w_s, b_s[:, :, None])


def ssm_operators(lam_re, lam_im, log_dt, b_re, b_im, c_re, c_im):
    g_all, p = lam_re.shape
    cg = SSM_GROUP
    L = SSM_CHUNK
    gpt = LANES // cg
    nt = g_all // gpt
    lr = jnp.minimum(lam_re, -1e-4)
    li = lam_im
    dt = jnp.exp(log_dt)[:, None]
    mag = jnp.exp(lr * dt)
    a_re = mag * jnp.cos(li * dt)
    a_im = mag * jnp.sin(li * dt)
    den = lr * lr + li * li
    nr = a_re - 1.0
    f_re = (nr * lr + a_im * li) / den
    f_im = (a_im * lr - nr * li) / den
    bb_re = f_re[:, :, None] * b_re - f_im[:, :, None] * b_im
    bb_im = f_re[:, :, None] * b_im + f_im[:, :, None] * b_re
    tau = jnp.arange(L + 1, dtype=F32)[:, None, None]
    pmag = jnp.exp(lr * dt * tau)
    pw_re = pmag * jnp.cos(li * dt * tau)
    pw_im = pmag * jnp.sin(li * dt * tau)
    ab_re = pw_re[..., None] * bb_re - pw_im[..., None] * bb_im
    ab_im = pw_re[..., None] * bb_im + pw_im[..., None] * bb_re
    kk = (jnp.einsum('gcp,lgpd->lgcd', c_re, ab_re[:L], precision=lax.Precision.HIGHEST)
          - jnp.einsum('gcp,lgpd->lgcd', c_im, ab_im[:L], precision=lax.Precision.HIGHEST))
    eye = jnp.eye(gpt, dtype=F32)
    k_c = jnp.einsum('ljgcd,gh->jlgdhc', kk.reshape(L, nt, gpt, cg, cg), eye).reshape(nt, L, LANES, LANES)
    def state_in(ab):
        v = ab[:L][::-1].reshape(L, nt, gpt, p, cg).transpose(1, 0, 2, 4, 3).reshape(nt, L, LANES, p)
        return jnp.concatenate([v] * (LANES // p), axis=-1)
    b_c = jnp.stack([state_in(ab_re), state_in(ab_im)], axis=2)
    q_re = pw_re[1:]
    q_im = pw_im[1:]
    co_re = c_re[None] * q_re[:, :, None, :] - c_im[None] * q_im[:, :, None, :]
    co_im = -(c_re[None] * q_im[:, :, None, :] + c_im[None] * q_re[:, :, None, :])
    def state_out(co):
        v = co.reshape(L, nt, gpt, cg, p).transpose(1, 0, 4, 2, 3).reshape(nt, L, p, LANES)
        return jnp.concatenate([v] * (LANES // p), axis=-2)
    c_c = jnp.stack([state_out(co_re), state_out(co_im)], axis=2)
    al_re = pw_re[L].reshape(nt, 1, gpt * p)
    al_im = pw_im[L].reshape(nt, 1, gpt * p)
    return k_c.astype(BF16), b_c.astype(BF16), c_c.astype(BF16), al_re, al_im


def _ssm_expand(kc_ref, bc_ref, cc_ref, m_s, b_s, c_s):
    L = kc_ref.shape[1]
    nstate_half = b_s.shape[1] // 2
    pairs = nstate_half // LANES
    row = lax.broadcasted_iota(jnp.int32, (LANES, LANES), 0)
    lane = lax.broadcasted_iota(jnp.int32, (LANES, LANES), 1)
    per_blk = LANES // SSM_STATE
    zero = jnp.zeros((LANES, LANES), BF16)

    @pl.when(pl.program_id(0) == 0)
    def _():
        m_s[...] = jnp.zeros_like(m_s)

    for s_in in range(L):
        for s_out in range(s_in, L):
            m_s[s_in * LANES:(s_in + 1) * LANES, s_out * LANES:(s_out + 1) * LANES] = kc_ref[0, s_out - s_in]
    for pi in range(pairs):
        in_mask = (row // SSM_GROUP) == (per_blk * pi + lane // SSM_STATE)
        out_mask = (lane // SSM_GROUP) == (per_blk * pi + row // SSM_STATE)
        for s in range(L):
            for ri in range(2):
                col = ri * nstate_half + pi * LANES
                b_s[s * LANES:(s + 1) * LANES, col:col + LANES] = jnp.where(in_mask, bc_ref[0, s, ri], zero)
                c_s[col:col + LANES, s * LANES:(s + 1) * LANES] = jnp.where(out_mask, cc_ref[0, s, ri], zero)


def _ssm_kernel(u_ref, kc_ref, bc_ref, cc_ref, are_ref, aim_ref, d_ref, o_ref,
                m_s, b_s, c_s, xl3, xp3, pw3, ybuf, *, bsz, nseg, seglen, pitch):
    L = kc_ref.shape[1]
    nblk = xl3.shape[0]
    hb = nblk // 2
    nseq = bsz * nseg
    _ssm_expand(kc_ref, bc_ref, cc_ref, m_s, b_s, c_s)

    @pl.when(pl.program_id(0) == 0)
    def _():
        xp3[...] = jnp.zeros_like(xp3)

    zpad = jnp.zeros((pitch - seglen, LANES), F32)
    u_steps = []
    for s in range(L):
        pieces = []
        for m in range(nseq):
            pieces += [u_ref[pl.ds(m * seglen * L + s, seglen, stride=L), :], zpad]
        u_steps.append(jnp.concatenate(pieces, axis=0))
    u2 = jnp.concatenate(u_steps, axis=1).astype(BF16)
    xl = _dot(u2, b_s[...])
    for kb in range(nblk):
        xl3[kb] = xl[:, kb * LANES:(kb + 1) * LANES]
    a_re = [are_ref[0, :, kb * LANES:(kb + 1) * LANES] for kb in range(hb)]
    a_im = [aim_ref[0, :, kb * LANES:(kb + 1) * LANES] for kb in range(hb)]

    seg_pow = []
    for kb in range(hb):
        pw3[kb, 0:1, :] = jnp.ones((1, LANES), F32)
        pw3[hb + kb, 0:1, :] = jnp.zeros((1, LANES), F32)
        qr, qi = a_re[kb], a_im[kb]
        n = 1
        while n < seglen:
            pr, pi = pw3[kb, 0:n, :], pw3[hb + kb, 0:n, :]
            pw3[kb, n:2 * n, :] = pr * qr - pi * qi
            pw3[hb + kb, n:2 * n, :] = pr * qi + pi * qr
            qr, qi = qr * qr - qi * qi, 2.0 * qr * qi
            n *= 2
        seg_pow.append((qr, qi))

    def step(k, xs):
        rows = pl.ds(k, nseq, stride=pitch)
        new_x = []
        for kb in range(hb):
            xr, xi = xs[kb]
            xp3.at[kb][rows, :] = xr
            xp3.at[hb + kb][rows, :] = xi
            lr = xl3.at[kb][rows, :]
            li = xl3.at[hb + kb][rows, :]
            new_x.append((a_re[kb] * xr - a_im[kb] * xi + lr, a_re[kb] * xi + a_im[kb] * xr + li))
        return tuple(new_x)

    x0 = tuple((jnp.zeros((nseq, LANES), F32), jnp.zeros((nseq, LANES), F32)) for _ in range(hb))
    xs = lax.fori_loop(0, seglen, step, x0)

    for kb in range(hb):
        pr = pw3[kb]
        pi = pw3[hb + kb]
        fr, fi = xs[kb]
        qr, qi = seg_pow[kb]
        for b in range(bsz):
            sr = si = None
            for q in range(1, nseg):
                m = b * nseg + q
                er, ei = fr[m - 1:m], fi[m - 1:m]
                if sr is None:
                    sr, si = er, ei
                else:
                    sr, si = er + qr * sr - qi * si, ei + qr * si + qi * sr
                sl = slice(m * pitch, m * pitch + seglen)
                xp3[kb, sl, :] = xp3[kb, sl, :] + (pr * sr - pi * si)
                xp3[hb + kb, sl, :] = xp3[hb + kb, sl, :] + (pr * si + pi * sr)

    xp = jnp.concatenate([xp3[kb] for kb in range(nblk)], axis=1).astype(BF16)
    y = _dot(u2, m_s[...]) + _dot(xp, c_s[...])
    d = d_ref[...]
    for s in range(L):
        ys = _gelu(y[:, s * LANES:(s + 1) * LANES] + d * u_steps[s])
        for m in range(nseq):
            ybuf[pl.ds(m * seglen * L + s, seglen, stride=L), :] = ys[m * pitch:m * pitch + seglen]
    o_ref[...] = ybuf[...].astype(o_ref.dtype)


def ssm_mixer(u, ops, d_skip, bsz):
    k_c, b_c, c_c, al_re, al_im = ops
    t, width = u.shape
    L = k_c.shape[1]
    nrow = t // L
    nt = width // LANES
    nstate = 2 * al_re.shape[2]
    sublanes = 8
    nseg = sublanes // bsz
    seglen = nrow // (bsz * nseg)
    nblk = nstate // LANES
    assert seglen & (seglen - 1) == 0 and nrow == bsz * nseg * seglen
    pitch = seglen + sublanes
    prow = bsz * nseg * pitch
    kern = functools.partial(_ssm_kernel, bsz=bsz, nseg=nseg, seglen=seglen, pitch=pitch)
    return pl.pallas_call(
        kern,
        grid=(nt,),
        in_specs=[pl.BlockSpec((t, LANES), lambda j: (0, j)),
                  pl.BlockSpec((1, L, LANES, LANES), lambda j: (j, 0, 0, 0)),
                  pl.BlockSpec((1, L, 2, LANES, LANES), lambda j: (j, 0, 0, 0, 0)),
                  pl.BlockSpec((1, L, 2, LANES, LANES), lambda j: (j, 0, 0, 0, 0)),
                  pl.BlockSpec((1, 1, nstate // 2), lambda j: (j, 0, 0)),
                  pl.BlockSpec((1, 1, nstate // 2), lambda j: (j, 0, 0)),
                  pl.BlockSpec((1, LANES), lambda j: (0, j))],
        out_specs=pl.BlockSpec((t, LANES), lambda j: (0, j)),
        out_shape=jax.ShapeDtypeStruct((t, width), BF16),
        scratch_shapes=[pltpu.VMEM((L * LANES, L * LANES), BF16), pltpu.VMEM((L * LANES, nstate), BF16),
                        pltpu.VMEM((nstate, L * LANES), BF16),
                        pltpu.VMEM((nblk, prow, LANES), F32), pltpu.VMEM((nblk, prow, LANES), F32),
                        pltpu.VMEM((nblk, seglen, LANES), F32), pltpu.VMEM((t, LANES), F32)],
        compiler_params=_params("arbitrary"),
        name="ssm_mixer",
    )(u, k_c, b_c, c_c, al_re, al_im, d_skip)


def _route_kernel(logit_ref, idx_ref, gate_ref, *, n_experts):
    lg = logit_ref[...]
    lane = lax.broadcasted_iota(jnp.int32, lg.shape, 1)
    neg = jnp.float32(-jnp.inf)
    lg = jnp.where(lane < n_experts, lg, neg)
    m1 = jnp.max(lg, axis=-1, keepdims=True)
    i1 = jnp.min(jnp.where(lg == m1, lane, LANES), axis=-1, keepdims=True)
    lg2 = jnp.where(lane == i1, neg, lg)
    m2 = jnp.max(lg2, axis=-1, keepdims=True)
    i2 = jnp.min(jnp.where(lg2 == m2, lane, LANES), axis=-1, keepdims=True)
    e2 = jnp.exp(m2 - m1)
    p1 = 1.0 / (1.0 + e2)
    p2 = e2 / (1.0 + e2)
    idx_ref[...] = jnp.where(lane == 0, i1, jnp.where(lane == 1, i2, 0))
    gate_ref[...] = jnp.where(lane == 0, p1, jnp.where(lane == 1, p2, 0.0))


def route_top2(logits, *, tm=512):
    t = logits.shape[0]
    kern = functools.partial(_route_kernel, n_experts=N_EXPERTS)
    return pl.pallas_call(
        kern,
        grid=(t // tm,),
        in_specs=[pl.BlockSpec((tm, LANES), lambda i: (i, 0))],
        out_specs=[pl.BlockSpec((tm, LANES), lambda i: (i, 0)), pl.BlockSpec((tm, LANES), lambda i: (i, 0))],
        out_shape=[jax.ShapeDtypeStruct((t, LANES), jnp.int32), jax.ShapeDtypeStruct((t, LANES), F32)],
        compiler_params=_params("arbitrary"),
    )(logits)


def _dispatch_kernel(tok_ref, h_hbm, o_ref, buf, sems, *, tile):
    i = pl.program_id(0)

    def row_copy(step, r):
        slot = step % 2
        return pltpu.make_async_copy(h_hbm.at[pl.ds(tok_ref[step * tile + r], 1)],
                                     buf.at[slot, pl.ds(r, 1)], sems.at[slot])

    def issue_step(step):
        def issue(k, c):
            row_copy(step, 2 * k).start(priority=0)
            row_copy(step, 2 * k + 1).start(priority=1)
            return c
        lax.fori_loop(0, tile // 2, issue, 0)

    @pl.when(i == 0)
    def _():
        issue_step(i)

    @pl.when(i + 1 < pl.num_programs(0))
    def _():
        issue_step(i + 1)

    def drain(r, c):
        row_copy(i, r).wait()
        return c

    lax.fori_loop(0, tile, drain, 0)
    o_ref[...] = buf[i % 2].astype(o_ref.dtype)


def moe_dispatch(h, tok_of_slot, n_slots, *, tile=MOE_TILE):
    d = h.shape[1]
    kern = functools.partial(_dispatch_kernel, tile=tile)
    return pl.pallas_call(
        kern,
        grid_spec=pltpu.PrefetchScalarGridSpec(
            num_scalar_prefetch=1,
            grid=(n_slots // tile,),
            in_specs=[pl.BlockSpec(memory_space=pl.ANY)],
            out_specs=pl.BlockSpec((tile, d), lambda i, tok: (i, 0)),
            scratch_shapes=[pltpu.VMEM((2, tile, d), F32), pltpu.SemaphoreType.DMA((2,))]),
        out_shape=jax.ShapeDtypeStruct((n_slots, d), BF16),
        compiler_params=_params("arbitrary"),
        name="moe_dispatch",
    )(tok_of_slot, h)


def _stream_expert_weights(te_ref, nre_ref, w_hbms, stages, wbfs, sems, tn):
    j = pl.program_id(0)
    i = pl.program_id(1)
    e = te_ref[i]
    run_start = jnp.logical_or(i == 0, e != te_ref[jnp.maximum(i - 1, 0)])

    def copies(ex, jx):
        col = pl.multiple_of(jx * tn, tn)
        return [pltpu.make_async_copy(w.at[ex, :, pl.ds(col, tn)], st, sems.at[k])
                for k, (w, st) in enumerate(zip(w_hbms, stages))]

    @pl.when(run_start)
    def _():
        @pl.when(jnp.logical_and(j == 0, i == 0))
        def _():
            for cp in copies(e, j):
                cp.start()

        for cp in copies(e, j):
            cp.wait()
        for st, wb in zip(stages, wbfs):
            wb[...] = st[...].astype(BF16)
        nxt = nre_ref[i]
        last = nxt < 0
        e_next = jnp.where(last, te_ref[0], nxt)
        j_next = jnp.where(last, j + 1, j)

        @pl.when(jnp.logical_not(jnp.logical_and(last, j == pl.num_programs(0) - 1)))
        def _():
            for cp in copies(e_next, j_next):
                cp.start()


def _moe_up_kernel(te_ref, na_ref, nre_ref, a_ref, wg_hbm, wu_hbm, o_ref, stg, stu, wgbf, wubf, sems, *, tn):
    _stream_expert_weights(te_ref, nre_ref, (wg_hbm, wu_hbm), (stg, stu), (wgbf, wubf), sems, tn)

    @pl.when(pl.program_id(1) < na_ref[0])
    def _():
        a = a_ref[...]
        g = _dot(a, wgbf[...])
        u = _dot(a, wubf[...])
        o_ref[...] = (g * jax.nn.sigmoid(g) * u).astype(o_ref.dtype)

    @pl.when(pl.program_id(1) >= na_ref[0])
    def _():
        o_ref[...] = jnp.zeros_like(o_ref)


def moe_up(hg, plan, wg, wu, *, tile=MOE_TILE, tn=512):
    tile_expert, n_active, next_run_expert = plan
    r, k = hg.shape
    n = wg.shape[2]
    tn = min(tn, n)
    kern = functools.partial(_moe_up_kernel, tn=tn)
    return pl.pallas_call(
        kern,
        grid_spec=pltpu.PrefetchScalarGridSpec(
            num_scalar_prefetch=3,
            grid=(n // tn, r // tile),
            in_specs=[pl.BlockSpec((tile, k), lambda j, i, te, na, nre: (i, 0)),
                      pl.BlockSpec(memory_space=pl.ANY),
                      pl.BlockSpec(memory_space=pl.ANY)],
            out_specs=pl.BlockSpec((tile, tn), lambda j, i, te, na, nre: (i, j)),
            scratch_shapes=[pltpu.VMEM((k, tn), F32), pltpu.VMEM((k, tn), F32),
                            pltpu.VMEM((k, tn), BF16), pltpu.VMEM((k, tn), BF16),
                            pltpu.SemaphoreType.DMA((2,))]),
        out_shape=jax.ShapeDtypeStruct((r, n), BF16),
        compiler_params=_params("arbitrary", "arbitrary"),
        name="moe_up",
    )(tile_expert, n_active, next_run_expert, hg, wg, wu)


def _moe_down_kernel(te_ref, na_ref, nre_ref, a_ref, w_hbm, o_ref, stage, wbf, sems, *, tn):
    _stream_expert_weights(te_ref, nre_ref, (w_hbm,), (stage,), (wbf,), sems, tn)

    @pl.when(pl.program_id(1) < na_ref[0])
    def _():
        o_ref[...] = _dot(a_ref[...], wbf[...])

    @pl.when(pl.program_id(1) >= na_ref[0])
    def _():
        o_ref[...] = jnp.zeros_like(o_ref)


def moe_down(act, plan, wd, *, tile=MOE_TILE, tn=1024):
    tile_expert, n_active, next_run_expert = plan
    r, k = act.shape
    n = wd.shape[2]
    tn = min(tn, n)
    kern = functools.partial(_moe_down_kernel, tn=tn)
    return pl.pallas_call(
        kern,
        grid_spec=pltpu.PrefetchScalarGridSpec(
            num_scalar_prefetch=3,
            grid=(n // tn, r // tile),
            in_specs=[pl.BlockSpec((tile, k), lambda j, i, te, na, nre: (i, 0)),
                      pl.BlockSpec(memory_space=pl.ANY)],
            out_specs=pl.BlockSpec((tile, tn), lambda j, i, te, na, nre: (i, j)),
            scratch_shapes=[pltpu.VMEM((k, tn), F32), pltpu.VMEM((k, tn), BF16),
                            pltpu.SemaphoreType.DMA((1,))]),
        out_shape=jax.ShapeDtypeStruct((r, n), F32),
        compiler_params=_params("arbitrary", "arbitrary"),
        name="moe_down",
    )(tile_expert, n_active, next_run_expert, act, wd)


def _combine_kernel(pos_ref, og_hbm, x_ref, gate_ref, mod_ref, fg_ref, o_ref, buf, sems, *, tm):
    i = pl.program_id(0)

    def copies(step, r):
        slot = step % 2
        t = step * tm + r
        return tuple(pltpu.make_async_copy(og_hbm.at[pl.ds(pos_ref[2 * t + c], 1)],
                                           buf.at[slot, c, pl.ds(r, 1)], sems.at[slot, c]) for c in range(2))

    def issue_step(step):
        def issue(r, carry):
            for c, cp in enumerate(copies(step, r)):
                cp.start(priority=c)
            return carry
        lax.fori_loop(0, tm, issue, 0)

    @pl.when(i == 0)
    def _():
        issue_step(i)

    @pl.when(i + 1 < pl.num_programs(0))
    def _():
        issue_step(i + 1)

    def drain(r, carry):
        for cp in copies(i, r):
            cp.wait()
        return carry

    lax.fori_loop(0, tm, drain, 0)
    g = gate_ref[...]
    slot = i % 2
    moe = g[:, 0:1] * buf[slot, 0] + g[:, 1:2] * buf[slot, 1]
    xn = x_ref[...] + mod_ref[0, 2:3, :] * moe
    o_ref[...] = _rms(xn) * fg_ref[...]


def moe_combine_final(og, pos_flat, x, gates, mod, final_g, seq, *, tm=128):
    t, d = x.shape
    per_b = seq // tm
    kern = functools.partial(_combine_kernel, tm=tm)
    return pl.pallas_call(
        kern,
        grid_spec=pltpu.PrefetchScalarGridSpec(
            num_scalar_prefetch=1,
            grid=(t // tm,),
            in_specs=[pl.BlockSpec(memory_space=pl.ANY),
                      pl.BlockSpec((tm, d), lambda i, pos: (i, 0)),
                      pl.BlockSpec((tm, LANES), lambda i, pos: (i, 0)),
                      pl.BlockSpec((1, 3, d), lambda i, pos: (i // per_b, 0, 0)),
                      pl.BlockSpec((1, d), lambda i, pos: (0, 0))],
            out_specs=pl.BlockSpec((tm, d), lambda i, pos: (i, 0)),
            scratch_shapes=[pltpu.VMEM((2, 2, tm, d), F32), pltpu.SemaphoreType.DMA((2, 2))]),
        out_shape=jax.ShapeDtypeStruct((t, d), F32),
        compiler_params=_params("arbitrary"),
        name="moe_combine",
    )(pos_flat, og, x, gates, mod, final_g)


def moe_plan(idx, n_tokens, *, tile=MOE_TILE):
    e1 = idx[:, 0]
    e2 = idx[:, 1]
    flat_e = jnp.stack([e1, e2], axis=1).reshape(-1)
    onehot = (flat_e[:, None] == jnp.arange(N_EXPERTS)[None, :]).astype(jnp.int32)
    rank = jnp.cumsum(onehot, axis=0) - onehot
    counts = jnp.sum(onehot, axis=0)
    padded = ((counts + tile - 1) // tile) * tile
    ends = jnp.cumsum(padded)
    starts = ends - padded
    pos = jnp.sum(onehot * (rank + starts[None, :]), axis=1)
    n_slots = 2 * n_tokens + N_EXPERTS * tile
    n_tiles = n_slots // tile
    tok = jnp.repeat(jnp.arange(n_tokens, dtype=jnp.int32), 2)
    tok_of_slot = jnp.zeros((n_slots,), jnp.int32).at[pos].set(tok)
    n_active = (ends[-1] // tile).astype(jnp.int32)
    tile_start = jnp.arange(n_tiles, dtype=jnp.int32) * tile
    tile_expert = jnp.sum((tile_start[:, None] >= ends[None, :]).astype(jnp.int32), axis=1)
    last_expert = tile_expert[jnp.maximum(n_active - 1, 0)]
    tile_expert = jnp.where(jnp.arange(n_tiles) < n_active, tile_expert, last_expert).astype(jnp.int32)
    ids = jnp.arange(N_EXPERTS)
    later_present = jnp.logical_and(ids[None, :] > ids[:, None], (counts > 0)[None, :])
    next_present = jnp.min(jnp.where(later_present, ids[None, :], N_EXPERTS), axis=1)
    next_present = jnp.where(next_present == N_EXPERTS, -1, next_present).astype(jnp.int32)
    next_run_expert = next_present[tile_expert]
    plan = (tile_expert, n_active.reshape(1), next_run_expert)
    return pos.astype(jnp.int32), tok_of_slot, plan, n_slots


def kernel(x, c, mix0_norm_g, mix0_ada_w, mix0_ada_b, mix0_w_in, gm_ln_g, gm_w_s, gm_b_s, mix0_w_out, ffn0_norm_g, ffn0_ada_w, ffn0_ada_b, ffn0_w_gate, ffn0_w_up, ffn0_w_down, mix1_norm_g, mix1_ada_w, mix1_ada_b, ssm_w_in, ssm_lam_re, ssm_lam_im, ssm_log_dt, ssm_b_re, ssm_b_im, ssm_c_re, ssm_c_im, ssm_d, glu_w_a, glu_w_b, moe_norm_g, moe_ada_w, moe_ada_b, moe_w_router, moe_w_gate, moe_w_up, moe_w_down, final_norm_g):
    bsz, seq, d = x.shape
    t = bsz * seq
    n_heads = mix0_w_in.shape[2] // (5 * SB_HEAD_DIM)
    n_groups = gm_w_s.shape[1]
    xf = x.reshape(t, d)
    c_pad = jnp.zeros((8, d), F32).at[:bsz].set(c)
    tm_big = min(1024, seq)

    def ada(w, b):
        m = ada_params(c_pad, w[0], b[0][None, :])
        return m[:bsz].reshape(bsz, 3, d)

    mod = ada(mix0_ada_w, mix0_ada_b)
    h = norm_mod(xf, mix0_norm_g, mod, seq)
    qkvz = matmul(h, mix0_w_in[0], tm=tm_big,tn=512, out_dtype=BF16)
    a_out = stick_breaking_attention(qkvz, bsz, seq, n_heads)
    b_out = spatial_gating(qkvz, gm_ln_g[0], gm_w_s[0], gm_b_s[0], n_groups)
    xf = matmul2_resid(a_out, b_out, mix0_w_out[0], xf, mod, seq, tm=tm_big,tn=512)
    mod = ada(ffn0_ada_w, ffn0_ada_b)
    h = norm_mod(xf, ffn0_norm_g, mod, seq)
    act = matmul_swiglu(h, ffn0_w_gate[0], ffn0_w_up[0], tm=tm_big,tn=256)
    xf = matmul_resid(act, ffn0_w_down[0], xf, mod, seq, tm=256, tn=512)
    mod = ada(mix1_ada_w, mix1_ada_b)
    h = norm_mod(xf, mix1_norm_g, mod, seq)
    u = matmul(h, ssm_w_in[0], tm=tm_big,tn=512, out_dtype=F32)
    ops = ssm_operators(ssm_lam_re[0], ssm_lam_im[0], ssm_log_dt[0], ssm_b_re[0], ssm_b_im[0],
                        ssm_c_re[0], ssm_c_im[0])
    y = ssm_mixer(u, ops, ssm_d, bsz)
    xf = matmul_glu_resid(y, glu_w_a[0], glu_w_b[0], xf, mod, seq, tm=tm_big,tn=256)
    mod = ada(moe_ada_w, moe_ada_b)
    w_router_pad = jnp.zeros((d, LANES), F32).at[:, :N_EXPERTS].set(moe_w_router[0])
    h32, logits = norm_router(xf, moe_norm_g, mod, w_router_pad, seq)
    idx, gates = route_top2(logits)
    pos, tok_of_slot, plan, n_slots = moe_plan(idx, t)
    hg = moe_dispatch(h32, tok_of_slot, n_slots)
    act = moe_up(hg, plan, moe_w_gate[0], moe_w_up[0])
    og = moe_down(act, plan, moe_w_down[0])
    out = moe_combine_final(og, pos, xf, gates, mod, final_norm_g[None, :], seq)
    return out.reshape(bsz, seq, d)
```

```python
import functools
import math

import jax
import jax.numpy as jnp
from jax import lax
from jax.experimental import pallas as pl
from jax.experimental.pallas import tpu as pltpu

EPS = 1e-6
SB_HEAD_DIM = 128
SB_TILE = 256
GM_GROUP_DIM = 128
GM_CHUNK = 128
GM_ROWS = 2048
SSM_GROUP = 16
SSM_STATE = 64
SSM_CHUNK = 8
LANES = 128
N_EXPERTS = 8
MOE_TILE = 256
SB_SKIP_EXPONENT = 104.0
SB_MASKED_SCORE = -1e30
VMEM_LIMIT_BYTES = 56 * 1024 * 1024

BF16 = jnp.bfloat16
F32 = jnp.float32


def _params(*sem):
    return pltpu.CompilerParams(dimension_semantics=sem, vmem_limit_bytes=VMEM_LIMIT_BYTES)


def _dot(a, b):
    return jnp.dot(a, b, preferred_element_type=F32)


def _ada_kernel(c_ref, w_ref, b_ref, o_ref):
    c = c_ref[...]
    s = (c * jax.nn.sigmoid(c))
    s_hi = s.astype(BF16)
    s_lo = (s - s_hi.astype(F32)).astype(BF16)
    w = w_ref[...].astype(BF16)
    o_ref[...] = _dot(s_hi, w) + _dot(s_lo, w) + b_ref[...]


def ada_params(c_pad, w, b, *, tn=512):
    rows, d = c_pad.shape
    n = w.shape[1]
    return pl.pallas_call(
        _ada_kernel,
        grid=(n // tn,),
        in_specs=[pl.BlockSpec((rows, d), lambda j: (0, 0)),
                  pl.BlockSpec((d, tn), lambda j: (0, j)),
                  pl.BlockSpec((1, tn), lambda j: (0, j))],
        out_specs=pl.BlockSpec((rows, tn), lambda j: (0, j)),
        out_shape=jax.ShapeDtypeStruct((rows, n), F32),
        compiler_params=_params("arbitrary"),
        name="ada_params",
    )(c_pad, w, b)


def _rms(x):
    return x * lax.rsqrt(jnp.mean(x * x, axis=-1, keepdims=True) + EPS)


def _norm_mod_kernel(x_ref, g_ref, mod_ref, o_ref):
    y = _rms(x_ref[...]) * g_ref[...]
    o_ref[...] = (y * (1.0 + mod_ref[0, 1:2, :]) + mod_ref[0, 0:1, :]).astype(o_ref.dtype)


def norm_mod(x, g, mod, seq, *, tm=512, out_dtype=BF16):
    t, d = x.shape
    per_b = seq // tm
    return pl.pallas_call(
        _norm_mod_kernel,
        grid=(t // tm,),
        in_specs=[pl.BlockSpec((tm, d), lambda i: (i, 0)),
                  pl.BlockSpec((1, d), lambda i: (0, 0)),
                  pl.BlockSpec((1, 3, d), lambda i: (i // per_b, 0, 0))],
        out_specs=pl.BlockSpec((tm, d), lambda i: (i, 0)),
        out_shape=jax.ShapeDtypeStruct((t, d), out_dtype),
        compiler_params=_params("arbitrary"),
        name="norm_mod",
    )(x, g, mod)


def _norm_router_kernel(x_ref, g_ref, mod_ref, wr_ref, h_ref, logit_ref):
    y = _rms(x_ref[...]) * g_ref[...]
    h = y * (1.0 + mod_ref[0, 1:2, :]) + mod_ref[0, 0:1, :]
    h_ref[...] = h
    logit_ref[...] = jnp.dot(h, wr_ref[...], preferred_element_type=F32, precision=lax.Precision.HIGHEST)


def norm_router(x, g, mod, w_router_pad, seq, *, tm=512):
    t, d = x.shape
    per_b = seq // tm
    return pl.pallas_call(
        _norm_router_kernel,
        grid=(t // tm,),
        in_specs=[pl.BlockSpec((tm, d), lambda i: (i, 0)),
                  pl.BlockSpec((1, d), lambda i: (0, 0)),
                  pl.BlockSpec((1, 3, d), lambda i: (i // per_b, 0, 0)),
                  pl.BlockSpec((d, LANES), lambda i: (0, 0))],
        out_specs=[pl.BlockSpec((tm, d), lambda i: (i, 0)),
                   pl.BlockSpec((tm, LANES), lambda i: (i, 0))],
        out_shape=[jax.ShapeDtypeStruct((t, d), F32), jax.ShapeDtypeStruct((t, LANES), F32)],
        compiler_params=_params("arbitrary"),
        name="norm_router",
    )(x, g, mod, w_router_pad)


def _cast_once(w_ref, wbf_ref):
    @pl.when(pl.program_id(1) == 0)
    def _():
        wbf_ref[...] = w_ref[...].astype(BF16)


def _mm_kernel(a_ref, w_ref, o_ref, wbf):
    _cast_once(w_ref, wbf)
    o_ref[...] = _dot(a_ref[...], wbf[...]).astype(o_ref.dtype)


def matmul(a, w, *, tm, tn, out_dtype):
    m, k = a.shape
    n = w.shape[1]
    return pl.pallas_call(
        _mm_kernel,
        grid=(n // tn, m // tm),
        in_specs=[pl.BlockSpec((tm, k), lambda j, i: (i, 0)),
                  pl.BlockSpec((k, tn), lambda j, i: (0, j))],
        out_specs=pl.BlockSpec((tm, tn), lambda j, i: (i, j)),
        out_shape=jax.ShapeDtypeStruct((m, n), out_dtype),
        scratch_shapes=[pltpu.VMEM((k, tn), BF16)],
        compiler_params=_params("arbitrary", "arbitrary"),
        name="mm",
    )(a, w)


def _mm_resid_kernel(a_ref, w_hbm, x_ref, mod_ref, o_ref, stage, wbf, sem, *, tn):
    j = pl.program_id(0)

    def tile_copy(jx):
        return pltpu.make_async_copy(w_hbm.at[:, pl.ds(pl.multiple_of(jx * tn, tn), tn)], stage, sem.at[0])

    @pl.when(pl.program_id(1) == 0)
    def _():
        @pl.when(j == 0)
        def _():
            tile_copy(j).start()

        tile_copy(j).wait()
        wbf[...] = stage[...].astype(BF16)

        @pl.when(j + 1 < pl.num_programs(0))
        def _():
            tile_copy(j + 1).start()

    o_ref[...] = x_ref[...] + mod_ref[0, 2:3, :] * _dot(a_ref[...], wbf[...])


def matmul_resid(a, w, x, mod, seq, *, tm, tn):
    m, k = a.shape
    n = w.shape[1]
    per_b = seq // tm
    return pl.pallas_call(
        functools.partial(_mm_resid_kernel, tn=tn),
        grid=(n // tn, m // tm),
        in_specs=[pl.BlockSpec((tm, k), lambda j, i: (i, 0)),
                  pl.BlockSpec(memory_space=pl.ANY),
                  pl.BlockSpec((tm, tn), lambda j, i: (i, j)),
                  pl.BlockSpec((1, 3, tn), lambda j, i: (i // per_b, 0, j))],
        out_specs=pl.BlockSpec((tm, tn), lambda j, i: (i, j)),
        out_shape=jax.ShapeDtypeStruct((m, n), F32),
        scratch_shapes=[pltpu.VMEM((k, tn), F32), pltpu.VMEM((k, tn), BF16), pltpu.SemaphoreType.DMA((1,))],
        compiler_params=_params("arbitrary", "arbitrary"),
        name="mm_resid",
    )(a, w, x, mod)


def _mm2_resid_kernel(a_ref, b_ref, w_ref, x_ref, mod_ref, o_ref, wbf):
    _cast_once(w_ref, wbf)
    ka = a_ref.shape[1]
    acc = _dot(a_ref[...], wbf[:ka, :]) + _dot(b_ref[...], wbf[ka:, :])
    o_ref[...] = x_ref[...] + mod_ref[0, 2:3, :] * acc


def matmul2_resid(a, b, w, x, mod, seq, *, tm, tn):
    m, ka = a.shape
    kb = b.shape[1]
    n = w.shape[1]
    per_b = seq // tm
    return pl.pallas_call(
        _mm2_resid_kernel,
        grid=(n // tn, m // tm),
        in_specs=[pl.BlockSpec((tm, ka), lambda j, i: (i, 0)),
                  pl.BlockSpec((tm, kb), lambda j, i: (i, 0)),
                  pl.BlockSpec((ka + kb, tn), lambda j, i: (0, j)),
                  pl.BlockSpec((tm, tn), lambda j, i: (i, j)),
                  pl.BlockSpec((1, 3, tn), lambda j, i: (i // per_b, 0, j))],
        out_specs=pl.BlockSpec((tm, tn), lambda j, i: (i, j)),
        out_shape=jax.ShapeDtypeStruct((m, n), F32),
        scratch_shapes=[pltpu.VMEM((ka + kb, tn), BF16)],
        compiler_params=_params("arbitrary", "arbitrary"),
        name="mm2_resid",
    )(a, b, w, x, mod)


def _cast2_once(w1_ref, w2_ref, w1bf, w2bf):
    @pl.when(pl.program_id(1) == 0)
    def _():
        w1bf[...] = w1_ref[...].astype(BF16)
        w2bf[...] = w2_ref[...].astype(BF16)


def _mm_swiglu_kernel(a_ref, wg_ref, wu_ref, o_ref, wgbf, wubf):
    _cast2_once(wg_ref, wu_ref, wgbf, wubf)
    a = a_ref[...]
    g = _dot(a, wgbf[...])
    u = _dot(a, wubf[...])
    o_ref[...] = (g * jax.nn.sigmoid(g) * u).astype(o_ref.dtype)


def matmul_swiglu(a, wg, wu, *, tm, tn):
    m, k = a.shape
    n = wg.shape[1]
    return pl.pallas_call(
        _mm_swiglu_kernel,
        grid=(n // tn, m // tm),
        in_specs=[pl.BlockSpec((tm, k), lambda j, i: (i, 0)),
                  pl.BlockSpec((k, tn), lambda j, i: (0, j)),
                  pl.BlockSpec((k, tn), lambda j, i: (0, j))],
        out_specs=pl.BlockSpec((tm, tn), lambda j, i: (i, j)),
        out_shape=jax.ShapeDtypeStruct((m, n), BF16),
        scratch_shapes=[pltpu.VMEM((k, tn), BF16), pltpu.VMEM((k, tn), BF16)],
        compiler_params=_params("arbitrary", "arbitrary"),
        name="mm_swiglu",
    )(a, wg, wu)


def _mm_glu_resid_kernel(a_ref, wa_ref, wb_ref, x_ref, mod_ref, o_ref, wabf, wbbf):
    _cast2_once(wa_ref, wb_ref, wabf, wbbf)
    a = a_ref[...]
    p = _dot(a, wabf[...])
    q = _dot(a, wbbf[...])
    o_ref[...] = x_ref[...] + mod_ref[0, 2:3, :] * (p * jax.nn.sigmoid(q))


def matmul_glu_resid(a, wa, wb, x, mod, seq, *, tm, tn):
    m, k = a.shape
    n = wa.shape[1]
    per_b = seq // tm
    return pl.pallas_call(
        _mm_glu_resid_kernel,
        grid=(n // tn, m // tm),
        in_specs=[pl.BlockSpec((tm, k), lambda j, i: (i, 0)),
                  pl.BlockSpec((k, tn), lambda j, i: (0, j)),
                  pl.BlockSpec((k, tn), lambda j, i: (0, j)),
                  pl.BlockSpec((tm, tn), lambda j, i: (i, j)),
                  pl.BlockSpec((1, 3, tn), lambda j, i: (i // per_b, 0, j))],
        out_specs=pl.BlockSpec((tm, tn), lambda j, i: (i, j)),
        out_shape=jax.ShapeDtypeStruct((m, n), F32),
        scratch_shapes=[pltpu.VMEM((k, tn), BF16), pltpu.VMEM((k, tn), BF16)],
        compiler_params=_params("arbitrary", "arbitrary"),
        name="mm_glu",
    )(a, wa, wb, x, mod)


def _sb_kernel(q_ref, k_ref, v_ref, o_ref, *, tile, scale, heads):
    qi = pl.program_id(2)
    dh = q_ref.shape[1] // heads
    row = lax.broadcasted_iota(jnp.int32, (tile, tile), 0)
    col = lax.broadcasted_iota(jnp.int32, (tile, tile), 1)
    revcum = jnp.where(row >= col, 1.0, 0.0).astype(BF16)
    key_minus_query = col - row

    def cond(carry):
        step, _, min_later = carry
        return jnp.logical_and(step <= qi, min_later <= SB_SKIP_EXPONENT)

    def body(carry):
        step, state, _ = carry
        kj = qi - step
        start = pl.multiple_of(kj * tile, tile)
        past = key_minus_query < step * tile
        hs = range(heads)
        lanes = [slice(h * dh, (h + 1) * dh) for h in hs]
        zs = [lax.dot_general(q_ref[:, lanes[h]], k_ref[pl.ds(start, tile), lanes[h]], (((1,), (1,)), ((), ())),
                              preferred_element_type=F32) * scale for h in hs]
        zs = [jnp.where(past, z, SB_MASKED_SCORE) for z in zs]
        sps = [jnp.maximum(z, 0.0) + jnp.log(1.0 + jnp.exp(-jnp.abs(z))) for z in zs]
        his = [sp.astype(BF16) for sp in sps]
        los = [(sp - hi.astype(F32)).astype(BF16) for sp, hi in zip(sps, his)]
        cums = [_dot(hi, revcum) + _dot(lo, revcum) for hi, lo in zip(his, los)]
        ws = [jnp.exp(z - cum - state[h][1]).astype(BF16) for h, (z, cum) in enumerate(zip(zs, cums))]
        accs = [state[h][0] + _dot(ws[h], v_ref[pl.ds(start, tile), lanes[h]]) for h in hs]
        laters = [state[h][1] + jnp.sum(sps[h], axis=1, keepdims=True) for h in hs]
        min_later = functools.reduce(jnp.minimum, [jnp.min(later) for later in laters])
        return step + 1, tuple(zip(accs, laters)), min_later

    state0 = tuple((jnp.zeros((tile, dh), F32), jnp.zeros((tile, 1), F32)) for _ in range(heads))
    _, state, _ = lax.while_loop(cond, body, (jnp.int32(0), state0, jnp.float32(0.0)))
    for h in range(heads):
        o_ref[:, h * dh:(h + 1) * dh] = state[h][0].astype(o_ref.dtype)


def stick_breaking_attention(qkvz, bsz, seq, n_heads, *, tile=SB_TILE, heads=4):
    t = qkvz.shape[0]
    dh = SB_HEAD_DIM
    nq = seq // tile
    hg = n_heads // heads
    kern = functools.partial(_sb_kernel, tile=tile, scale=1.0 / math.sqrt(dh), heads=heads)
    return pl.pallas_call(
        kern,
        grid=(bsz, hg, nq),
        in_specs=[pl.BlockSpec((tile, heads * dh), lambda b, h, i: (b * nq + i, h)),
                  pl.BlockSpec((seq, heads * dh), lambda b, h, i: (b, hg + h)),
                  pl.BlockSpec((seq, heads * dh), lambda b, h, i: (b, 2 * hg + h))],
        out_specs=pl.BlockSpec((tile, heads * dh), lambda b, h, i: (b * nq + i, h)),
        out_shape=jax.ShapeDtypeStruct((t, n_heads * dh), BF16),
        compiler_params=_params("arbitrary", "arbitrary", "arbitrary"),
        name="sb_attention",
    )(qkvz, qkvz, qkvz)


def _gelu(x):
    return jax.nn.gelu(x, approximate=True)


def _gm_kernel(z1_ref, z2_ref, lng_ref, ws_ref, bs_ref, o_ref, *, chunk):
    u = _gelu(z1_ref[...].astype(F32))
    v = _gelu(z2_ref[...].astype(F32))
    mu = jnp.mean(v, axis=-1, keepdims=True)
    vc = v - mu
    var = jnp.mean(vc * vc, axis=-1, keepdims=True)
    vn = vc * lax.rsqrt(var + EPS) * lng_ref[0]
    row = lax.broadcasted_iota(jnp.int32, (chunk, chunk), 0)
    col = lax.broadcasted_iota(jnp.int32, (chunk, chunk), 1)
    w = jnp.where(row >= col, ws_ref[0], 0.0)
    bias = bs_ref[0]
    for ci in range(u.shape[0] // chunk):
        sl = slice(ci * chunk, (ci + 1) * chunk)
        mixed = jnp.dot(w, vn[sl], preferred_element_type=F32, precision=lax.Precision.HIGHEST) + bias
        o_ref[sl, :] = (u[sl] * mixed).astype(o_ref.dtype)


def spatial_gating(qkvz, ln_g, w_s, b_s, n_groups, *, rows=GM_ROWS):
    t = qkvz.shape[0]
    c = GM_GROUP_DIM
    rows = min(rows, t)
    kern = functools.partial(_gm_kernel, chunk=GM_CHUNK)
    return pl.pallas_call(
        kern,
        grid=(n_groups, t // rows),
        in_specs=[pl.BlockSpec((rows, c), lambda g, i: (i, 3 * n_groups + g)),
                  pl.BlockSpec((rows, c), lambda g, i: (i, 4 * n_groups + g)),
                  pl.BlockSpec((1, 1, c), lambda g, i: (g, 0, 0)),
                  pl.BlockSpec((1, GM_CHUNK, GM_CHUNK), lambda g, i: (g, 0, 0)),
                  pl.BlockSpec((1, GM_CHUNK, 1), lambda g, i: (g, 0, 0))],
        out_specs=pl.BlockSpec((rows, c), lambda g, i: (i, g)),
        out_shape=jax.ShapeDtypeStruct((t, n_groups * c), BF16),
        compiler_params=_params("arbitrary", "arbitrary"),
        name="spatial_gating",
    )(qkvz, qkvz, ln_g[:, None, :], w_s, b_s[:, :, None])


def ssm_operators(lam_re, lam_im, log_dt, b_re, b_im, c_re, c_im):
    g_all, p = lam_re.shape
    cg = SSM_GROUP
    L = SSM_CHUNK
    gpt = LANES // cg
    nt = g_all // gpt
    lr = jnp.minimum(lam_re, -1e-4)
    li = lam_im
    dt = jnp.exp(log_dt)[:, None]
    mag = jnp.exp(lr * dt)
    a_re = mag * jnp.cos(li * dt)
    a_im = mag * jnp.sin(li * dt)
    den = lr * lr + li * li
    nr = a_re - 1.0
    f_re = (nr * lr + a_im * li) / den
    f_im = (a_im * lr - nr * li) / den
    bb_re = f_re[:, :, None] * b_re - f_im[:, :, None] * b_im
    bb_im = f_re[:, :, None] * b_im + f_im[:, :, None] * b_re
    tau = jnp.arange(L + 1, dtype=F32)[:, None, None]
    pmag = jnp.exp(lr * dt * tau)
    pw_re = pmag * jnp.cos(li * dt * tau)
    pw_im = pmag * jnp.sin(li * dt * tau)
    ab_re = pw_re[..., None] * bb_re - pw_im[..., None] * bb_im
    ab_im = pw_re[..., None] * bb_im + pw_im[..., None] * bb_re
    kk = (jnp.einsum('gcp,lgpd->lgcd', c_re, ab_re[:L], precision=lax.Precision.HIGHEST)
          - jnp.einsum('gcp,lgpd->lgcd', c_im, ab_im[:L], precision=lax.Precision.HIGHEST))
    eye = jnp.eye(gpt, dtype=F32)
    k_c = jnp.einsum('ljgcd,gh->jlgdhc', kk.reshape(L, nt, gpt, cg, cg), eye).reshape(nt, L, LANES, LANES)
    def state_in(ab):
        v = ab[:L][::-1].reshape(L, nt, gpt, p, cg).transpose(1, 0, 2, 4, 3).reshape(nt, L, LANES, p)
        return jnp.concatenate([v] * (LANES // p), axis=-1)
    b_c = jnp.stack([state_in(ab_re), state_in(ab_im)], axis=2)
    q_re = pw_re[1:]
    q_im = pw_im[1:]
    co_re = c_re[None] * q_re[:, :, None, :] - c_im[None] * q_im[:, :, None, :]
    co_im = -(c_re[None] * q_im[:, :, None, :] + c_im[None] * q_re[:, :, None, :])
    def state_out(co):
        v = co.reshape(L, nt, gpt, cg, p).transpose(1, 0, 4, 2, 3).reshape(nt, L, p, LANES)
        return jnp.concatenate([v] * (LANES // p), axis=-2)
    c_c = jnp.stack([state_out(co_re), state_out(co_im)], axis=2)
    al_re = pw_re[L].reshape(nt, 1, gpt * p)
    al_im = pw_im[L].reshape(nt, 1, gpt * p)
    return k_c.astype(BF16), b_c.astype(BF16), c_c.astype(BF16), al_re, al_im


def _ssm_expand(kc_ref, bc_ref, cc_ref, m_s, b_s, c_s):
    L = kc_ref.shape[1]
    nstate_half = b_s.shape[1] // 2
    pairs = nstate_half // LANES
    row = lax.broadcasted_iota(jnp.int32, (LANES, LANES), 0)
    lane = lax.broadcasted_iota(jnp.int32, (LANES, LANES), 1)
    per_blk = LANES // SSM_STATE
    zero = jnp.zeros((LANES, LANES), BF16)

    @pl.when(pl.program_id(0) == 0)
    def _():
        m_s[...] = jnp.zeros_like(m_s)

    for s_in in range(L):
        for s_out in range(s_in, L):
            m_s[s_in * LANES:(s_in + 1) * LANES, s_out * LANES:(s_out + 1) * LANES] = kc_ref[0, s_out - s_in]
    for pi in range(pairs):
        in_mask = (row // SSM_GROUP) == (per_blk * pi + lane // SSM_STATE)
        out_mask = (lane // SSM_GROUP) == (per_blk * pi + row // SSM_STATE)
        for s in range(L):
            for ri in range(2):
                col = ri * nstate_half + pi * LANES
                b_s[s * LANES:(s + 1) * LANES, col:col + LANES] = jnp.where(in_mask, bc_ref[0, s, ri], zero)
                c_s[col:col + LANES, s * LANES:(s + 1) * LANES] = jnp.where(out_mask, cc_ref[0, s, ri], zero)


def _ssm_kernel(u_ref, kc_ref, bc_ref, cc_ref, are_ref, aim_ref, d_ref, o_ref,
                m_s, b_s, c_s, xl3, xp3, pw3, ybuf, *, bsz, nseg, seglen, pitch):
    L = kc_ref.shape[1]
    nblk = xl3.shape[0]
    hb = nblk // 2
    nseq = bsz * nseg
    _ssm_expand(kc_ref, bc_ref, cc_ref, m_s, b_s, c_s)

    @pl.when(pl.program_id(0) == 0)
    def _():
        xp3[...] = jnp.zeros_like(xp3)

    zpad = jnp.zeros((pitch - seglen, LANES), F32)
    u_steps = []
    for s in range(L):
        pieces = []
        for m in range(nseq):
            pieces += [u_ref[pl.ds(m * seglen * L + s, seglen, stride=L), :], zpad]
        u_steps.append(jnp.concatenate(pieces, axis=0))
    u2 = jnp.concatenate(u_steps, axis=1).astype(BF16)
    xl = _dot(u2, b_s[...])
    for kb in range(nblk):
        xl3[kb] = xl[:, kb * LANES:(kb + 1) * LANES]
    a_re = [are_ref[0, :, kb * LANES:(kb + 1) * LANES] for kb in range(hb)]
    a_im = [aim_ref[0, :, kb * LANES:(kb + 1) * LANES] for kb in range(hb)]

    seg_pow = []
    for kb in range(hb):
        pw3[kb, 0:1, :] = jnp.ones((1, LANES), F32)
        pw3[hb + kb, 0:1, :] = jnp.zeros((1, LANES), F32)
        qr, qi = a_re[kb], a_im[kb]
        n = 1
        while n < seglen:
            pr, pi = pw3[kb, 0:n, :], pw3[hb + kb, 0:n, :]
            pw3[kb, n:2 * n, :] = pr * qr - pi * qi
            pw3[hb + kb, n:2 * n, :] = pr * qi + pi * qr
            qr, qi = qr * qr - qi * qi, 2.0 * qr * qi
            n *= 2
        seg_pow.append((qr, qi))

    def step(k, xs):
        rows = pl.ds(k, nseq, stride=pitch)
        new_x = []
        for kb in range(hb):
            xr, xi = xs[kb]
            xp3.at[kb][rows, :] = xr
            xp3.at[hb + kb][rows, :] = xi
            lr = xl3.at[kb][rows, :]
            li = xl3.at[hb + kb][rows, :]
            new_x.append((a_re[kb] * xr - a_im[kb] * xi + lr, a_re[kb] * xi + a_im[kb] * xr + li))
        return tuple(new_x)

    x0 = tuple((jnp.zeros((nseq, LANES), F32), jnp.zeros((nseq, LANES), F32)) for _ in range(hb))
    xs = lax.fori_loop(0, seglen, step, x0)

    for kb in range(hb):
        pr = pw3[kb]
        pi = pw3[hb + kb]
        fr, fi = xs[kb]
        qr, qi = seg_pow[kb]
        for b in range(bsz):
            sr = si = None
            for q in range(1, nseg):
                m = b * nseg + q
                er, ei = fr[m - 1:m], fi[m - 1:m]
                if sr is None:
                    sr, si = er, ei
                else:
                    sr, si = er + qr * sr - qi * si, ei + qr * si + qi * sr
                sl = slice(m * pitch, m * pitch + seglen)
                xp3[kb, sl, :] = xp3[kb, sl, :] + (pr * sr - pi * si)
                xp3[hb + kb, sl, :] = xp3[hb + kb, sl, :] + (pr * si + pi * sr)

    xp = jnp.concatenate([xp3[kb] for kb in range(nblk)], axis=1).astype(BF16)
    y = _dot(u2, m_s[...]) + _dot(xp, c_s[...])
    d = d_ref[...]
    for s in range(L):
        ys = _gelu(y[:, s * LANES:(s + 1) * LANES] + d * u_steps[s])
        for m in range(nseq):
            ybuf[pl.ds(m * seglen * L + s, seglen, stride=L), :] = ys[m * pitch:m * pitch + seglen]
    o_ref[...] = ybuf[...].astype(o_ref.dtype)


def ssm_mixer(u, ops, d_skip, bsz):
    k_c, b_c, c_c, al_re, al_im = ops
    t, width = u.shape
    L = k_c.shape[1]
    nrow = t // L
    nt = width // LANES
    nstate = 2 * al_re.shape[2]
    sublanes = 8
    nseg = sublanes // bsz
    seglen = nrow // (bsz * nseg)
    nblk = nstate // LANES
    assert seglen & (seglen - 1) == 0 and nrow == bsz * nseg * seglen
    pitch = seglen + sublanes
    prow = bsz * nseg * pitch
    kern = functools.partial(_ssm_kernel, bsz=bsz, nseg=nseg, seglen=seglen, pitch=pitch)
    return pl.pallas_call(
        kern,
        grid=(nt,),
        in_specs=[pl.BlockSpec((t, LANES), lambda j: (0, j)),
                  pl.BlockSpec((1, L, LANES, LANES), lambda j: (j, 0, 0, 0)),
                  pl.BlockSpec((1, L, 2, LANES, LANES), lambda j: (j, 0, 0, 0, 0)),
                  pl.BlockSpec((1, L, 2, LANES, LANES), lambda j: (j, 0, 0, 0, 0)),
                  pl.BlockSpec((1, 1, nstate // 2), lambda j: (j, 0, 0)),
                  pl.BlockSpec((1, 1, nstate // 2), lambda j: (j, 0, 0)),
                  pl.BlockSpec((1, LANES), lambda j: (0, j))],
        out_specs=pl.BlockSpec((t, LANES), lambda j: (0, j)),
        out_shape=jax.ShapeDtypeStruct((t, width), BF16),
        scratch_shapes=[pltpu.VMEM((L * LANES, L * LANES), BF16), pltpu.VMEM((L * LANES, nstate), BF16),
                        pltpu.VMEM((nstate, L * LANES), BF16),
                        pltpu.VMEM((nblk, prow, LANES), F32), pltpu.VMEM((nblk, prow, LANES), F32),
                        pltpu.VMEM((nblk, seglen, LANES), F32), pltpu.VMEM((t, LANES), F32)],
        compiler_params=_params("arbitrary"),
        name="ssm_mixer",
    )(u, k_c, b_c, c_c, al_re, al_im, d_skip)


def _route_kernel(logit_ref, idx_ref, gate_ref, *, n_experts):
    lg = logit_ref[...]
    lane = lax.broadcasted_iota(jnp.int32, lg.shape, 1)
    neg = jnp.float32(-jnp.inf)
    lg = jnp.where(lane < n_experts, lg, neg)
    m1 = jnp.max(lg, axis=-1, keepdims=True)
    i1 = jnp.min(jnp.where(lg == m1, lane, LANES), axis=-1, keepdims=True)
    lg2 = jnp.where(lane == i1, neg, lg)
    m2 = jnp.max(lg2, axis=-1, keepdims=True)
    i2 = jnp.min(jnp.where(lg2 == m2, lane, LANES), axis=-1, keepdims=True)
    e2 = jnp.exp(m2 - m1)
    p1 = 1.0 / (1.0 + e2)
    p2 = e2 / (1.0 + e2)
    idx_ref[...] = jnp.where(lane == 0, i1, jnp.where(lane == 1, i2, 0))
    gate_ref[...] = jnp.where(lane == 0, p1, jnp.where(lane == 1, p2, 0.0))


def route_top2(logits, *, tm=512):
    t = logits.shape[0]
    kern = functools.partial(_route_kernel, n_experts=N_EXPERTS)
    return pl.pallas_call(
        kern,
        grid=(t // tm,),
        in_specs=[pl.BlockSpec((tm, LANES), lambda i: (i, 0))],
        out_specs=[pl.BlockSpec((tm, LANES), lambda i: (i, 0)), pl.BlockSpec((tm, LANES), lambda i: (i, 0))],
        out_shape=[jax.ShapeDtypeStruct((t, LANES), jnp.int32), jax.ShapeDtypeStruct((t, LANES), F32)],
        compiler_params=_params("arbitrary"),
    )(logits)


def _dispatch_kernel(tok_ref, h_hbm, o_ref, buf, sems, *, tile):
    i = pl.program_id(0)

    def row_copy(step, r):
        slot = step % 2
        return pltpu.make_async_copy(h_hbm.at[pl.ds(tok_ref[step * tile + r], 1)],
                                     buf.at[slot, pl.ds(r, 1)], sems.at[slot])

    def issue_step(step):
        def issue(k, c):
            row_copy(step, 2 * k).start(priority=0)
            row_copy(step, 2 * k + 1).start(priority=1)
            return c
        lax.fori_loop(0, tile // 2, issue, 0)

    @pl.when(i == 0)
    def _():
        issue_step(i)

    @pl.when(i + 1 < pl.num_programs(0))
    def _():
        issue_step(i + 1)

    def drain(r, c):
        row_copy(i, r).wait()
        return c

    lax.fori_loop(0, tile, drain, 0)
    o_ref[...] = buf[i % 2].astype(o_ref.dtype)


def moe_dispatch(h, tok_of_slot, n_slots, *, tile=MOE_TILE):
    d = h.shape[1]
    kern = functools.partial(_dispatch_kernel, tile=tile)
    return pl.pallas_call(
        kern,
        grid_spec=pltpu.PrefetchScalarGridSpec(
            num_scalar_prefetch=1,
            grid=(n_slots // tile,),
            in_specs=[pl.BlockSpec(memory_space=pl.ANY)],
            out_specs=pl.BlockSpec((tile, d), lambda i, tok: (i, 0)),
            scratch_shapes=[pltpu.VMEM((2, tile, d), F32), pltpu.SemaphoreType.DMA((2,))]),
        out_shape=jax.ShapeDtypeStruct((n_slots, d), BF16),
        compiler_params=_params("arbitrary"),
        name="moe_dispatch",
    )(tok_of_slot, h)


def _stream_expert_weights(te_ref, nre_ref, w_hbms, stages, wbfs, sems, tn):
    j = pl.program_id(0)
    i = pl.program_id(1)
    e = te_ref[i]
    run_start = jnp.logical_or(i == 0, e != te_ref[jnp.maximum(i - 1, 0)])

    def copies(ex, jx):
        col = pl.multiple_of(jx * tn, tn)
        return [pltpu.make_async_copy(w.at[ex, :, pl.ds(col, tn)], st, sems.at[k])
                for k, (w, st) in enumerate(zip(w_hbms, stages))]

    @pl.when(run_start)
    def _():
        @pl.when(jnp.logical_and(j == 0, i == 0))
        def _():
            for cp in copies(e, j):
                cp.start()

        for cp in copies(e, j):
            cp.wait()
        for st, wb in zip(stages, wbfs):
            wb[...] = st[...].astype(BF16)
        nxt = nre_ref[i]
        last = nxt < 0
        e_next = jnp.where(last, te_ref[0], nxt)
        j_next = jnp.where(last, j + 1, j)

        @pl.when(jnp.logical_not(jnp.logical_and(last, j == pl.num_programs(0) - 1)))
        def _():
            for cp in copies(e_next, j_next):
                cp.start()


def _moe_up_kernel(te_ref, na_ref, nre_ref, a_ref, wg_hbm, wu_hbm, o_ref, stg, stu, wgbf, wubf, sems, *, tn):
    _stream_expert_weights(te_ref, nre_ref, (wg_hbm, wu_hbm), (stg, stu), (wgbf, wubf), sems, tn)

    @pl.when(pl.program_id(1) < na_ref[0])
    def _():
        a = a_ref[...]
        g = _dot(a, wgbf[...])
        u = _dot(a, wubf[...])
        o_ref[...] = (g * jax.nn.sigmoid(g) * u).astype(o_ref.dtype)

    @pl.when(pl.program_id(1) >= na_ref[0])
    def _():
        o_ref[...] = jnp.zeros_like(o_ref)


def moe_up(hg, plan, wg, wu, *, tile=MOE_TILE, tn=512):
    tile_expert, n_active, next_run_expert = plan
    r, k = hg.shape
    n = wg.shape[2]
    tn = min(tn, n)
    kern = functools.partial(_moe_up_kernel, tn=tn)
    return pl.pallas_call(
        kern,
        grid_spec=pltpu.PrefetchScalarGridSpec(
            num_scalar_prefetch=3,
            grid=(n // tn, r // tile),
            in_specs=[pl.BlockSpec((tile, k), lambda j, i, te, na, nre: (i, 0)),
                      pl.BlockSpec(memory_space=pl.ANY),
                      pl.BlockSpec(memory_space=pl.ANY)],
            out_specs=pl.BlockSpec((tile, tn), lambda j, i, te, na, nre: (i, j)),
            scratch_shapes=[pltpu.VMEM((k, tn), F32), pltpu.VMEM((k, tn), F32),
                            pltpu.VMEM((k, tn), BF16), pltpu.VMEM((k, tn), BF16),
                            pltpu.SemaphoreType.DMA((2,))]),
        out_shape=jax.ShapeDtypeStruct((r, n), BF16),
        compiler_params=_params("arbitrary", "arbitrary"),
        name="moe_up",
    )(tile_expert, n_active, next_run_expert, hg, wg, wu)


def _moe_down_kernel(te_ref, na_ref, nre_ref, a_ref, w_hbm, o_ref, stage, wbf, sems, *, tn):
    _stream_expert_weights(te_ref, nre_ref, (w_hbm,), (stage,), (wbf,), sems, tn)

    @pl.when(pl.program_id(1) < na_ref[0])
    def _():
        o_ref[...] = _dot(a_ref[...], wbf[...])

    @pl.when(pl.program_id(1) >= na_ref[0])
    def _():
        o_ref[...] = jnp.zeros_like(o_ref)


def moe_down(act, plan, wd, *, tile=MOE_TILE, tn=1024):
    tile_expert, n_active, next_run_expert = plan
    r, k = act.shape
    n = wd.shape[2]
    tn = min(tn, n)
    kern = functools.partial(_moe_down_kernel, tn=tn)
    return pl.pallas_call(
        kern,
        grid_spec=pltpu.PrefetchScalarGridSpec(
            num_scalar_prefetch=3,
            grid=(n // tn, r // tile),
            in_specs=[pl.BlockSpec((tile, k), lambda j, i, te, na, nre: (i, 0)),
                      pl.BlockSpec(memory_space=pl.ANY)],
            out_specs=pl.BlockSpec((tile, tn), lambda j, i, te, na, nre: (i, j)),
            scratch_shapes=[pltpu.VMEM((k, tn), F32), pltpu.VMEM((k, tn), BF16),
                            pltpu.SemaphoreType.DMA((1,))]),
        out_shape=jax.ShapeDtypeStruct((r, n), F32),
        compiler_params=_params("arbitrary", "arbitrary"),
        name="moe_down",
    )(tile_expert, n_active, next_run_expert, act, wd)


def _combine_kernel(pos_ref, og_hbm, x_ref, gate_ref, mod_ref, fg_ref, o_ref, buf, sems, *, tm):
    i = pl.program_id(0)

    def copies(step, r):
        slot = step % 2
        t = step * tm + r
        return tuple(pltpu.make_async_copy(og_hbm.at[pl.ds(pos_ref[2 * t + c], 1)],
                                           buf.at[slot, c, pl.ds(r, 1)], sems.at[slot, c]) for c in range(2))

    def issue_step(step):
        def issue(r, carry):
            for c, cp in enumerate(copies(step, r)):
                cp.start(priority=c)
            return carry
        lax.fori_loop(0, tm, issue, 0)

    @pl.when(i == 0)
    def _():
        issue_step(i)

    @pl.when(i + 1 < pl.num_programs(0))
    def _():
        issue_step(i + 1)

    def drain(r, carry):
        for cp in copies(i, r):
            cp.wait()
        return carry

    lax.fori_loop(0, tm, drain, 0)
    g = gate_ref[...]
    slot = i % 2
    moe = g[:, 0:1] * buf[slot, 0] + g[:, 1:2] * buf[slot, 1]
    xn = x_ref[...] + mod_ref[0, 2:3, :] * moe
    o_ref[...] = _rms(xn) * fg_ref[...]


def moe_combine_final(og, pos_flat, x, gates, mod, final_g, seq, *, tm=128):
    t, d = x.shape
    per_b = seq // tm
    kern = functools.partial(_combine_kernel, tm=tm)
    return pl.pallas_call(
        kern,
        grid_spec=pltpu.PrefetchScalarGridSpec(
            num_scalar_prefetch=1,
            grid=(t // tm,),
            in_specs=[pl.BlockSpec(memory_space=pl.ANY),
                      pl.BlockSpec((tm, d), lambda i, pos: (i, 0)),
                      pl.BlockSpec((tm, LANES), lambda i, pos: (i, 0)),
                      pl.BlockSpec((1, 3, d), lambda i, pos: (i // per_b, 0, 0)),
                      pl.BlockSpec((1, d), lambda i, pos: (0, 0))],
            out_specs=pl.BlockSpec((tm, d), lambda i, pos: (i, 0)),
            scratch_shapes=[pltpu.VMEM((2, 2, tm, d), F32), pltpu.SemaphoreType.DMA((2, 2))]),
        out_shape=jax.ShapeDtypeStruct((t, d), F32),
        compiler_params=_params("arbitrary"),
        name="moe_combine",
    )(pos_flat, og, x, gates, mod, final_g)


def moe_plan(idx, n_tokens, *, tile=MOE_TILE):
    e1 = idx[:, 0]
    e2 = idx[:, 1]
    flat_e = jnp.stack([e1, e2], axis=1).reshape(-1)
    onehot = (flat_e[:, None] == jnp.arange(N_EXPERTS)[None, :]).astype(jnp.int32)
    rank = jnp.cumsum(onehot, axis=0) - onehot
    counts = jnp.sum(onehot, axis=0)
    padded = ((counts + tile - 1) // tile) * tile
    ends = jnp.cumsum(padded)
    starts = ends - padded
    pos = jnp.sum(onehot * (rank + starts[None, :]), axis=1)
    n_slots = 2 * n_tokens + N_EXPERTS * tile
    n_tiles = n_slots // tile
    tok = jnp.repeat(jnp.arange(n_tokens, dtype=jnp.int32), 2)
    tok_of_slot = jnp.zeros((n_slots,), jnp.int32).at[pos].set(tok)
    n_active = (ends[-1] // tile).astype(jnp.int32)
    tile_start = jnp.arange(n_tiles, dtype=jnp.int32) * tile
    tile_expert = jnp.sum((tile_start[:, None] >= ends[None, :]).astype(jnp.int32), axis=1)
    last_expert = tile_expert[jnp.maximum(n_active - 1, 0)]
    tile_expert = jnp.where(jnp.arange(n_tiles) < n_active, tile_expert, last_expert).astype(jnp.int32)
    ids = jnp.arange(N_EXPERTS)
    later_present = jnp.logical_and(ids[None, :] > ids[:, None], (counts > 0)[None, :])
    next_present = jnp.min(jnp.where(later_present, ids[None, :], N_EXPERTS), axis=1)
    next_present = jnp.where(next_present == N_EXPERTS, -1, next_present).astype(jnp.int32)
    next_run_expert = next_present[tile_expert]
    plan = (tile_expert, n_active.reshape(1), next_run_expert)
    return pos.astype(jnp.int32), tok_of_slot, plan, n_slots


def kernel(x, c, mix0_norm_g, mix0_ada_w, mix0_ada_b, mix0_w_in, gm_ln_g, gm_w_s, gm_b_s, mix0_w_out, ffn0_norm_g, ffn0_ada_w, ffn0_ada_b, ffn0_w_gate, ffn0_w_up, ffn0_w_down, mix1_norm_g, mix1_ada_w, mix1_ada_b, ssm_w_in, ssm_lam_re, ssm_lam_im, ssm_log_dt, ssm_b_re, ssm_b_im, ssm_c_re, ssm_c_im, ssm_d, glu_w_a, glu_w_b, moe_norm_g, moe_ada_w, moe_ada_b, moe_w_router, moe_w_gate, moe_w_up, moe_w_down, final_norm_g):
    bsz, seq, d = x.shape
    t = bsz * seq
    n_heads = mix0_w_in.shape[2] // (5 * SB_HEAD_DIM)
    n_groups = gm_w_s.shape[1]
    xf = x.reshape(t, d)
    c_pad = jnp.zeros((8, d), F32).at[:bsz].set(c)
    tm_big = min(1024, seq)

    def ada(w, b):
        m = ada_params(c_pad, w[0], b[0][None, :])
        return m[:bsz].reshape(bsz, 3, d)

    mod = ada(mix0_ada_w, mix0_ada_b)
    h = norm_mod(xf, mix0_norm_g, mod, seq)
    qkvz = matmul(h, mix0_w_in[0], tm=tm_big,tn=512, out_dtype=BF16)
    a_out = stick_breaking_attention(qkvz, bsz, seq, n_heads)
    b_out = spatial_gating(qkvz, gm_ln_g[0], gm_w_s[0], gm_b_s[0], n_groups)
    xf = matmul2_resid(a_out, b_out, mix0_w_out[0], xf, mod, seq, tm=tm_big,tn=512)
    mod = ada(ffn0_ada_w, ffn0_ada_b)
    h = norm_mod(xf, ffn0_norm_g, mod, seq)
    act = matmul_swiglu(h, ffn0_w_gate[0], ffn0_w_up[0], tm=tm_big,tn=256)
    xf = matmul_resid(act, ffn0_w_down[0], xf, mod, seq, tm=256, tn=512)
    mod = ada(mix1_ada_w, mix1_ada_b)
    h = norm_mod(xf, mix1_norm_g, mod, seq)
    u = matmul(h, ssm_w_in[0], tm=tm_big,tn=512, out_dtype=F32)
    ops = ssm_operators(ssm_lam_re[0], ssm_lam_im[0], ssm_log_dt[0], ssm_b_re[0], ssm_b_im[0],
                        ssm_c_re[0], ssm_c_im[0])
    y = ssm_mixer(u, ops, ssm_d, bsz)
    xf = matmul_glu_resid(y, glu_w_a[0], glu_w_b[0], xf, mod, seq, tm=tm_big,tn=256)
    mod = ada(moe_ada_w, moe_ada_b)
    w_router_pad = jnp.zeros((d, LANES), F32).at[:, :N_EXPERTS].set(moe_w_router[0])
    h32, logits = norm_router(xf, moe_norm_g, mod, w_router_pad, seq)
    idx, gates = route_top2(logits)
    pos, tok_of_slot, plan, n_slots = moe_plan(idx, t)
    hg = moe_dispatch(h32, tok_of_slot, n_slots)
    act = moe_up(hg, plan, moe_w_gate[0], moe_w_up[0])
    og = moe_down(act, plan, moe_w_down[0])
    out = moe_combine_final(og, pos, xf, gates, mod, final_norm_g[None, :], seq)
    return out.reshape(bsz, seq, d)
```
